```python
import jax, jax.numpy as jnp
from jax import lax
import numpy as np

D_MODEL = 1024
BATCH = 16
SEQ = 2048
DEPTH = 1

GRID_W = 64
CTX_LEN = 256
FOURIER_WIDTH = 512
FOURIER_GROUPS = 4
FOURIER_GROUP_DIM = FOURIER_WIDTH // FOURIER_GROUPS
RWKV_WIDTH = 512
RWKV_HEAD_DIM = 64
RWKV_HEADS = RWKV_WIDTH // RWKV_HEAD_DIM
DECAY_LORA = 64
AAA_LORA = 64
GATE_LORA = 128
N_BRANCHES = 2
RWKV_SPLIT = (RWKV_WIDTH, RWKV_WIDTH, RWKV_WIDTH, DECAY_LORA, DECAY_LORA, AAA_LORA, AAA_LORA, GATE_LORA)
RWKV_COLS = sum(RWKV_SPLIT)
FOURIER_START = RWKV_COLS
GATE_START = RWKV_COLS + FOURIER_WIDTH
IN_COLS = GATE_START + N_BRANCHES * D_MODEL
D_FF = -(-8 * D_MODEL // (3 * 256)) * 256
NORM_EPS = 1e-6
GN_EPS = 64e-5

kernel_name = "hybrid_fourier_rwkv7_dit_block"


def _split(u, sizes):
    idx = [int(i) for i in np.cumsum(sizes)[:-1]]
    return jnp.split(u, idx, axis=-1)


def rmsnorm(u, g):
    uf = u.astype(jnp.float32)
    uf = uf * lax.rsqrt(jnp.mean(jnp.square(uf), axis=-1, keepdims=True) + NORM_EPS)
    return (uf * g.astype(jnp.float32)).astype(u.dtype)


def modulate(h, shift, scale):
    return h * (1.0 + scale) + shift


def centred_shift(u, mu_prev, mu_next):
    prev = jnp.pad(u, ((0, 0), (1, 0), (0, 0)))[:, :-1]
    nxt = jnp.pad(u, ((0, 0), (0, 1), (0, 0)))[:, 1:]
    return u + mu_prev * (prev - u) + mu_next * (nxt - u)


def _heads(t):
    return t.reshape(t.shape[0], t.shape[1], RWKV_HEADS, RWKV_HEAD_DIM)


def rwkv_inputs(u, lp):
    r, k, v, wd_f, wd_b, ad_f, ad_b, gd = _split(u, RWKV_SPLIT)
    kk = _heads(k * lp["k_k"]).astype(jnp.float32)
    kk = kk * lax.rsqrt(jnp.maximum(jnp.sum(jnp.square(kk), -1, keepdims=True), 1e-24))

    def direction(wd, w0, w2, ad, a0, a2):
        w = -jax.nn.softplus(-(w0 + jnp.tanh(wd) @ w2)) - 0.5
        decay = jnp.exp(-jnp.exp(w.astype(jnp.float32)))
        a = jax.nn.sigmoid(a0 + ad @ a2)
        kd = k * (1.0 + (a - 1.0) * lp["k_a"])
        return _heads(decay), _heads(a), _heads(kd)

    fwd = direction(wd_f, lp["w0_f"], lp["w2_f"], ad_f, lp["a0_f"], lp["a2_f"])
    bwd = direction(wd_b, lp["w0_b"], lp["w2_b"], ad_b, lp["a0_b"], lp["a2_b"])
    g = jax.nn.sigmoid(gd) @ lp["g2"]
    return {"r": _heads(r), "v": _heads(v), "kk": kk, "fwd": fwd, "bwd": bwd, "g": g}


def wkv_scan(r, decay, kd, v, kk, a, S0, reverse, emit):
    to_time = lambda t: jnp.moveaxis(t.astype(jnp.float32), 1, 0)
    xs = (to_time(r), to_time(decay), to_time(kd), to_time(v), to_time(kk), to_time(a))

    def step(S, inp):
        r_t, w_t, k_t, v_t, kk_t, a_t = inp
        sa = jnp.einsum("bhvk,bhk->bhv", S, -kk_t)
        S = (S * w_t[:, :, None, :] + sa[..., None] * (kk_t * a_t)[:, :, None, :]
             + v_t[..., None] * k_t[:, :, None, :])
        y = jnp.einsum("bhvk,bhk->bhv", S, r_t) if emit else None
        return S, y

    S, ys = lax.scan(step, S0, xs, reverse=reverse)
    return (jnp.moveaxis(ys, 0, 1) if emit else None), S


def bidir_wkv(q, Sf0, Sb0, emit):
    df, af, kf = q["fwd"]
    db, ab, kb = q["bwd"]
    yf, Sf = wkv_scan(q["r"], df, kf, q["v"], q["kk"], af, Sf0, False, emit)
    yb, Sb = wkv_scan(q["r"], db, kb, q["v"], q["kk"], ab, Sb0, True, emit)
    y = yf + yb if emit else None
    return y, Sf, Sb


def rwkv_output(y, q, lp):
    mean = jnp.mean(y, -1, keepdims=True)
    var = jnp.mean(jnp.square(y - mean), -1, keepdims=True)
    o = (y - mean) * lax.rsqrt(var + GN_EPS)
    o = o * lp["lnx_g"].reshape(RWKV_HEADS, RWKV_HEAD_DIM) + lp["lnx_b"].reshape(RWKV_HEADS, RWKV_HEAD_DIM)
    kd_sum = q["fwd"][2] + q["bwd"][2]
    bonus = jnp.sum(q["r"] * kd_sum * lp["r_k"], -1, keepdims=True) * q["v"]
    o = (o + bonus).astype(q["g"].dtype)
    o = o.reshape(o.shape[0], o.shape[1], RWKV_WIDTH) * q["g"]
    return o @ lp["w_up_r"]


def fourier_latent(u, rows):
    B = u.shape[0]
    uf = u.astype(jnp.float32).reshape(B, rows, GRID_W, FOURIER_GROUPS, FOURIER_GROUP_DIM)
    f = jnp.real(jnp.fft.fftn(uf, axes=(1, 2, 4), norm="ortho"))
    return f.reshape(B, rows * GRID_W, FOURIER_WIDTH).astype(u.dtype)


def fourier_context(u):
    B, L = u.shape[0], u.shape[1]
    uf = u.astype(jnp.float32).reshape(B, L, FOURIER_GROUPS, FOURIER_GROUP_DIM)
    f = jnp.real(jnp.fft.fftn(uf, axes=(1, 3), norm="ortho"))
    return f.reshape(B, L, FOURIER_WIDTH).astype(u.dtype)


def branch_merge(p, f_out, r_out, lp):
    gate_f, gate_r = _split(p[..., GATE_START:], (D_MODEL, D_MODEL))
    m = jax.nn.sigmoid(gate_f) * (f_out @ lp["w_up_f"]) + jax.nn.sigmoid(gate_r) * r_out
    return m @ lp["w_out"]


def swiglu(h, w_gu, w_down):
    gate, up = jnp.split(h @ w_gu, 2, axis=-1)
    return (jax.nn.silu(gate) * up) @ w_down


def trunk_layer(x, ctx, mod_x, mod_c, lp, update_ctx):
    rows = x.shape[1] // GRID_W
    B = x.shape[0]
    sh1, sc1, ga1, sh2, sc2, ga2 = jnp.split(mod_x, 6, axis=-1)
    csh1, csc1, cga1, csh2, csc2, cga2 = jnp.split(mod_c, 6, axis=-1)

    hx = modulate(rmsnorm(x, lp["norm1_g"]), sh1, sc1)
    px = hx @ lp["w_in"]
    qx = rwkv_inputs(centred_shift(px[..., :RWKV_COLS], lp["mu_prev"], lp["mu_next"]), lp)

    hc = modulate(rmsnorm(ctx, lp["norm1_g"]), csh1, csc1)
    pc = hc @ (lp["w_in"] if update_ctx else lp["w_in"][:, :RWKV_COLS])
    qc = rwkv_inputs(centred_shift(pc[..., :RWKV_COLS], lp["mu_prev"], lp["mu_next"]), lp)
    S0 = jnp.zeros((B, RWKV_HEADS, RWKV_HEAD_DIM, RWKV_HEAD_DIM), jnp.float32)
    yc, Sf, Sb = bidir_wkv(qc, S0, S0, update_ctx)
    yx, _, _ = bidir_wkv(qx, Sf, Sb, True)

    fx = fourier_latent(px[..., FOURIER_START:GATE_START], rows)
    x = x + ga1 * branch_merge(px, fx, rwkv_output(yx, qx, lp), lp)
    hx2 = modulate(rmsnorm(x, lp["norm2_g"]), sh2, sc2)
    x = x + ga2 * swiglu(hx2, lp["w_gu"], lp["w_down"])

    if update_ctx:
        fc = fourier_context(pc[..., FOURIER_START:GATE_START])
        ctx = ctx + cga1 * branch_merge(pc, fc, rwkv_output(yc, qc, lp), lp)
        hc2 = modulate(rmsnorm(ctx, lp["norm2_g"]), csh2, csc2)
        ctx = ctx + cga2 * swiglu(hc2, lp["w_gu"], lp["w_down"])
    return x, ctx


def setup_inputs(seed: int = 0) -> dict:
    key = jax.random.key(seed)
    ks = iter(jax.random.split(key, 40))
    L, D = DEPTH, D_MODEL

    def nrm(shape, scale):
        return scale * jax.random.normal(next(ks), shape, jnp.float32)

    def gain(shape):
        return 1.0 + nrm(shape, 0.02)

    ratio = jnp.linspace(0.0, 1.0, RWKV_WIDTH, dtype=jnp.float32)
    w0_base = -6.0 + 5.0 * ratio ** 0.9
    return {
        "x": nrm((BATCH, SEQ, D), 1.0),
        "c": nrm((BATCH, D), 1.0),
        "ctx": nrm((BATCH, CTX_LEN, D), 1.0),
        "c_ctx": nrm((D,), 1.0),
        "norm1_g": gain((L, D)),
        "norm2_g": gain((L, D)),
        "w_ada": nrm((L, D, 6 * D), D ** -0.5),
        "b_ada": nrm((L, 6 * D), 0.01),
        "w_in": nrm((L, D, IN_COLS), D ** -0.5),
        "mu_prev": jax.random.uniform(next(ks), (L, RWKV_COLS), jnp.float32, 0.0, 0.5),
        "mu_next": jax.random.uniform(next(ks), (L, RWKV_COLS), jnp.float32, 0.0, 0.5),
        "w0_f": w0_base + nrm((L, RWKV_WIDTH), 0.1),
        "w2_f": nrm((L, DECAY_LORA, RWKV_WIDTH), 0.5 * DECAY_LORA ** -0.5),
        "a0_f": nrm((L, RWKV_WIDTH), 0.1),
        "a2_f": nrm((L, AAA_LORA, RWKV_WIDTH), AAA_LORA ** -0.5),
        "w0_b": w0_base + nrm((L, RWKV_WIDTH), 0.1),
        "w2_b": nrm((L, DECAY_LORA, RWKV_WIDTH), 0.5 * DECAY_LORA ** -0.5),
        "a0_b": nrm((L, RWKV_WIDTH), 0.1),
        "a2_b": nrm((L, AAA_LORA, RWKV_WIDTH), AAA_LORA ** -0.5),
        "g2": nrm((L, GATE_LORA, RWKV_WIDTH), GATE_LORA ** -0.5),
        "k_k": 0.85 + nrm((L, RWKV_WIDTH), 0.02),
        "k_a": 1.0 + nrm((L, RWKV_WIDTH), 0.02),
        "r_k": nrm((L, RWKV_HEADS, RWKV_HEAD_DIM), 0.1),
        "lnx_g": gain((L, RWKV_WIDTH)),
        "lnx_b": nrm((L, RWKV_WIDTH), 0.01),
        "w_up_r": nrm((L, RWKV_WIDTH, D), RWKV_WIDTH ** -0.5),
        "w_up_f": nrm((L, FOURIER_WIDTH, D), FOURIER_WIDTH ** -0.5),
        "w_out": nrm((L, D, D), D ** -0.5),
        "w_gu": nrm((L, D, 2 * D_FF), D ** -0.5),
        "w_down": nrm((L, D_FF, D), D_FF ** -0.5),
        "final_norm_g": gain((D,)),
    }


def reference(x, c, ctx, c_ctx, norm1_g, norm2_g, w_ada, b_ada, w_in, mu_prev, mu_next,
              w0_f, w2_f, a0_f, a2_f, w0_b, w2_b, a0_b, a2_b, g2, k_k, k_a, r_k,
              lnx_g, lnx_b, w_up_r, w_up_f, w_out, w_gu, w_down, final_norm_g):
    for layer in range(DEPTH):
        lp = {
            "norm1_g": norm1_g[layer], "norm2_g": norm2_g[layer], "w_in": w_in[layer],
            "mu_prev": mu_prev[layer], "mu_next": mu_next[layer],
            "w0_f": w0_f[layer], "w2_f": w2_f[layer], "a0_f": a0_f[layer], "a2_f": a2_f[layer],
            "w0_b": w0_b[layer], "w2_b": w2_b[layer], "a0_b": a0_b[layer], "a2_b": a2_b[layer],
            "g2": g2[layer], "k_k": k_k[layer], "k_a": k_a[layer], "r_k": r_k[layer],
            "lnx_g": lnx_g[layer], "lnx_b": lnx_b[layer], "w_up_r": w_up_r[layer],
            "w_up_f": w_up_f[layer], "w_out": w_out[layer], "w_gu": w_gu[layer],
            "w_down": w_down[layer],
        }
        mod_x = (jax.nn.silu(c) @ w_ada[layer] + b_ada[layer])[:, None, :]
        mod_c = (jax.nn.silu(c_ctx) @ w_ada[layer] + b_ada[layer])[None, None, :]
        x, ctx = trunk_layer(x, ctx, mod_x, mod_c, lp, layer < DEPTH - 1)
    return rmsnorm(x, final_norm_g)
```

```python
import functools

import jax
import jax.numpy as jnp
import numpy as np
from jax import lax
from jax.experimental import pallas as pl
from jax.experimental.pallas import tpu as pltpu

D_MODEL = 1024
BATCH = 16
SEQ = 2048
GRID_W = 64
CTX_LEN = 256
FOURIER_WIDTH = 512
FOURIER_GROUPS = 4
FOURIER_GROUP_DIM = FOURIER_WIDTH // FOURIER_GROUPS
RWKV_WIDTH = 512
HEAD_DIM = 64
HEADS = RWKV_WIDTH // HEAD_DIM
DECAY_LORA = 64
AAA_LORA = 64
GATE_LORA = 128
RWKV_COLS = 3 * RWKV_WIDTH + 2 * DECAY_LORA + 2 * AAA_LORA + GATE_LORA
FOURIER_START = RWKV_COLS
GATE_START = RWKV_COLS + FOURIER_WIDTH
D_FF = 2816
NORM_EPS = 1e-6
GN_EPS = 64e-5

COL_R, COL_K, COL_V = 0, RWKV_WIDTH, 2 * RWKV_WIDTH
COL_WD = 3 * RWKV_WIDTH
COL_AD = COL_WD + 2 * DECAY_LORA
COL_GD = COL_AD + 2 * AAA_LORA

LANES = 128
SUBLANES = 8
TOK_TILE = 256
P_CORE = 3
P_SIDE = 1
CHUNK = 64
SEQ_ALL = CTX_LEN + SEQ
N_TILES = SEQ_ALL // TOK_TILE
N_LAT_TILES = SEQ // TOK_TILE
FFN_TILE = 512
FFN_SPLIT = 2
MOD_ROWS = 24
VMEM_LIMIT = 56 * 1024 * 1024

F32 = jnp.float32
BF16 = jnp.bfloat16


def _dot(a, b):
    return jnp.dot(a, b, preferred_element_type=F32)


def _dot_nt(a, b):
    return lax.dot_general(a, b, (((1,), (1,)), ((), ())), preferred_element_type=F32)


def _dot_tn(a, b):
    return lax.dot_general(a, b, (((0,), (0,)), ((), ())), preferred_element_type=F32)


_NN = (((1,), (0,)), ((), ()))
_NT = (((1,), (1,)), ((), ()))
_TN = (((0,), (0,)), ((), ()))


def _split(u):
    hi = u.astype(BF16)
    return hi, (u - hi.astype(F32)).astype(BF16)


def _mm(a, b, passes, dims=_NN):
    dg = functools.partial(lax.dot_general, dimension_numbers=dims, preferred_element_type=F32)
    if passes == 1:
        return dg(a.astype(BF16), b.astype(BF16))
    if passes == 6:
        return dg(a, b, precision=lax.Precision.HIGHEST)
    a_hi, a_lo = _split(a)
    b_hi, b_lo = _split(b)
    return dg(a_hi, b_hi) + dg(a_hi, b_lo) + dg(a_lo, b_hi)


def _dot_hilo(x, e):
    hi = x.astype(BF16)
    lo = (x - hi.astype(F32)).astype(BF16)
    return _dot(hi, e) + _dot(lo, e)


def _rms(u):
    return u * lax.rsqrt(jnp.mean(u * u, axis=-1, keepdims=True) + NORM_EPS)


def _sigmoid(z):
    return 1.0 / (1.0 + jnp.exp(-z))


def _softplus(z):
    return jnp.maximum(z, 0.0) + jnp.log(1.0 + jnp.exp(-jnp.abs(z)))


def _const_spec(shape):
    nd = len(shape)
    return pl.BlockSpec(shape, lambda *_: (0,) * nd, pipeline_mode=pl.Buffered(1))


def _mod_kernel(c_ref, w_ref, b_ref, o_ref):
    c = c_ref[...]
    s = (c * _sigmoid(c)).astype(BF16)
    o_ref[...] = _dot(s, w_ref[...]) + b_ref[...]


def _modulation(cc, w_ada, b_ada):
    n_blk = 4
    bn = (6 * D_MODEL) // n_blk
    return pl.pallas_call(
        _mod_kernel,
        out_shape=jax.ShapeDtypeStruct((MOD_ROWS, 6 * D_MODEL), F32),
        grid=(n_blk,),
        in_specs=[
            pl.BlockSpec((MOD_ROWS, D_MODEL), lambda n: (0, 0)),
            pl.BlockSpec((D_MODEL, bn), lambda n: (0, n)),
            pl.BlockSpec((1, bn), lambda n: (0, n)),
        ],
        out_specs=pl.BlockSpec((MOD_ROWS, bn), lambda n: (0, n)),
        name="adaln_mod",
    )(cc, w_ada, b_ada)


def _inproj_kernel(x_ref, xp_ref, xn_ref, ctx_ref, modb_ref, modc_ref, g1_ref, w_ref, mup_ref, mun_ref,
                   p_ref, xf_ref, pe_ref):
    j = pl.program_id(1)
    is_ctx = j == 0
    g1 = g1_ref[...]
    shift = jnp.where(is_ctx, modc_ref[0, :, 0:D_MODEL], modb_ref[0, :, 0:D_MODEL])
    scale = jnp.where(is_ctx, modc_ref[0, :, D_MODEL:2 * D_MODEL], modb_ref[0, :, D_MODEL:2 * D_MODEL])

    def norm_mod(u):
        return (_rms(u) * g1) * (1.0 + scale) + shift

    main = jnp.where(is_ctx, ctx_ref[0], x_ref[0])
    prev_ok = jnp.where(j >= 2, 1.0, 0.0)
    next_ok = jnp.where(jnp.logical_and(j >= 1, j <= N_LAT_TILES - 1), 1.0, 0.0)
    h_ext = jnp.concatenate(
        [norm_mod(xp_ref[0]) * prev_ok, norm_mod(main), norm_mod(xn_ref[0]) * next_ok], axis=0).astype(BF16)
    p_ext = _dot(h_ext, w_ref[...])
    xf_ref[0] = p_ext[SUBLANES:SUBLANES + TOK_TILE, FOURIER_START:GATE_START].astype(BF16)
    pe_ref[...] = p_ext[:, 0:RWKV_COLS]
    pm = pe_ref[pl.ds(SUBLANES, TOK_TILE), :]
    pu = pe_ref[pl.ds(SUBLANES - 1, TOK_TILE), :]
    pd = pe_ref[pl.ds(SUBLANES + 1, TOK_TILE), :]
    p_ref[0] = pm + mup_ref[...] * (pu - pm) + mun_ref[...] * (pd - pm)


def _inproj(x, ctx, mod3, g1, w_rf, mu_prev, mu_next):
    halo_blocks = TOK_TILE // SUBLANES
    last_halo = SEQ // SUBLANES - 1
    return pl.pallas_call(
        _inproj_kernel,
        out_shape=(jax.ShapeDtypeStruct((BATCH, SEQ_ALL, RWKV_COLS), F32),
                   jax.ShapeDtypeStruct((BATCH, SEQ, FOURIER_WIDTH), BF16)),
        grid=(BATCH, N_TILES),
        in_specs=[
            pl.BlockSpec((1, TOK_TILE, D_MODEL), lambda b, j: (b, jnp.maximum(j - 1, 0), 0)),
            pl.BlockSpec((1, SUBLANES, D_MODEL),
                         lambda b, j: (b, jnp.clip((j - 1) * halo_blocks - 1, 0, last_halo), 0)),
            pl.BlockSpec((1, SUBLANES, D_MODEL),
                         lambda b, j: (b, jnp.clip(j * halo_blocks, 0, last_halo), 0)),
            pl.BlockSpec((1, CTX_LEN, D_MODEL), lambda b, j: (b, 0, 0)),
            pl.BlockSpec((1, 1, 2 * D_MODEL), lambda b, j: (b, 0, 0)),
            pl.BlockSpec((1, 1, 2 * D_MODEL), lambda b, j: (BATCH, 0, 0)),
            _const_spec((1, D_MODEL)),
            _const_spec((D_MODEL, GATE_START)),
            _const_spec((1, RWKV_COLS)),
            _const_spec((1, RWKV_COLS)),
        ],
        out_specs=(
            pl.BlockSpec((1, TOK_TILE, RWKV_COLS), lambda b, j: (b, j, 0)),
            pl.BlockSpec((1, TOK_TILE, FOURIER_WIDTH), lambda b, j: (b, jnp.maximum(j - 1, 0), 0)),
        ),
        scratch_shapes=[pltpu.VMEM((TOK_TILE + 2 * SUBLANES, RWKV_COLS), F32)],
        compiler_params=pltpu.CompilerParams(
            dimension_semantics=("arbitrary", "arbitrary"), vmem_limit_bytes=VMEM_LIMIT),
        name="inproj_shift",
    )(x, x, x, ctx, mod3, mod3, g1, w_rf, mu_prev, mu_next)


_S_R, _S_V, _S_KD, _S_AL, _S_BE, _S_LD, _S_LGI = range(7)
_N_PLANES = 7
_V_W0F, _V_W0B, _V_A0F, _V_A0B, _V_KK, _V_KA, _V_RK = range(7)


def _wkv_kernel(pf_ref, pb_ref, wlo_ref, alo_ref, vec_ref, eh_ref, esel_ref, tri_ref,
                yf_ref, yb_ref, bsf_ref, bsb_ref, zt_ref, s_ref):
    j = pl.program_id(1)

    @pl.when(j == 0)
    def _():
        zt_ref[...] = jnp.zeros_like(zt_ref)

    k_k = vec_ref[_V_KK:_V_KK + 1, :]
    k_a = vec_ref[_V_KA:_V_KA + 1, :]
    r_k = vec_ref[_V_RK:_V_RK + 1, :]

    for d, (p_ref, bs_ref) in enumerate(((pf_ref, bsf_ref), (pb_ref, bsb_ref))):
        w0 = vec_ref[_V_W0F + d:_V_W0F + d + 1, :]
        a0 = vec_ref[_V_A0F + d:_V_A0F + d + 1, :]
        r = p_ref[0, :, COL_R:COL_R + RWKV_WIDTH]
        k = p_ref[0, :, COL_K:COL_K + RWKV_WIDTH]
        v = p_ref[0, :, COL_V:COL_V + RWKV_WIDTH]
        wa = p_ref[0, :, COL_WD:COL_WD + LANES]
        aa = p_ref[0, :, COL_AD:COL_AD + LANES]
        kq = k * k_k
        ss = _dot_hilo(kq * kq, eh_ref[...])
        kk = kq * lax.rsqrt(jnp.maximum(ss, 1e-24))
        wl = w0 + _dot(jnp.tanh(wa).astype(BF16), wlo_ref[d])
        ld = -jnp.exp(-_softplus(-wl) - 0.5)
        a = _sigmoid(a0 + _dot(aa.astype(BF16), alo_ref[d]))
        kd = k * (1.0 + (a - 1.0) * k_a)
        base = d * _N_PLANES
        s_ref[base + _S_R] = r
        s_ref[base + _S_V] = v
        s_ref[base + _S_KD] = kd
        s_ref[base + _S_AL] = -kk
        s_ref[base + _S_BE] = kk * a
        s_ref[base + _S_LD] = ld
        s_ref[base + _S_LGI] = jnp.dot(tri_ref[d], ld, precision=lax.Precision.HIGHEST,
                                       preferred_element_type=F32)
        bs_ref[0] = _dot_hilo(r * kd * r_k, esel_ref[...])

    n2 = 2 * CHUNK
    ri = lax.broadcasted_iota(jnp.int32, (n2, n2), 0)
    ci = lax.broadcasted_iota(jnp.int32, (n2, n2), 1)
    same_head = (ri >= CHUNK) == (ci >= CHUNK)
    tr = jnp.bitwise_and(ri, CHUNK - 1)
    tc = jnp.bitwise_and(ci, CHUNK - 1)
    eye = ri == ci
    strict = (jnp.logical_and(same_head, tc < tr), jnp.logical_and(same_head, tc > tr))
    incl = (jnp.logical_and(same_head, tc <= tr), jnp.logical_and(same_head, tc >= tr))
    lane_lo = lax.broadcasted_iota(jnp.int32, (CHUNK, LANES), 1) < HEAD_DIM
    merge_masks = [jnp.right_shift(ri, 1) == jnp.right_shift(ci, 1)]
    for sh in range(1, CHUNK.bit_length() - 1):
        merge_masks.append(jnp.logical_and(jnp.right_shift(ri, sh + 1) == jnp.right_shift(ci, sh + 1),
                                           jnp.right_shift(ri, sh) != jnp.right_shift(ci, sh)))

    def stack(u):
        return jnp.concatenate([jnp.where(lane_lo, u, 0.0), jnp.where(lane_lo, 0.0, u)], axis=0)

    def unit(d, p, r0, y_ref):
        base = d * _N_PLANES
        ls = slice(p * LANES, (p + 1) * LANES)
        rows = pl.ds(r0, CHUNK)
        ld = s_ref[base + _S_LD, rows, ls]
        lgi = s_ref[base + _S_LGI, rows, ls]
        m = lgi[CHUNK // 2:CHUNK // 2 + 1, :]
        last = CHUNK - 1 if d == 0 else 0
        lgc = lgi[last:last + 1, :]
        r = s_ref[base + _S_R, rows, ls]
        v = s_ref[base + _S_V, rows, ls]
        kd = s_ref[base + _S_KD, rows, ls]
        al = s_ref[base + _S_AL, rows, ls]
        be = s_ref[base + _S_BE, rows, ls]
        lge = lgi - ld
        a_t = al * jnp.exp(lge)
        a_m = al * jnp.exp(lge - m)
        r_t = r * jnp.exp(lgi)
        r_m = r * jnp.exp(lgi - m)
        e_m = jnp.exp(m - lgi)
        b_m = be * e_m
        k_m = kd * e_m
        e_c = jnp.exp(lgc - lgi)
        b_e = be * e_c
        k_e = kd * e_c
        g_c = jnp.exp(lgc)

        a2m = stack(a_m)
        r2m = stack(r_m)
        bb = jnp.concatenate([b_m, b_m], axis=0)
        kk2 = jnp.concatenate([k_m, k_m], axis=0)
        l_ab = jnp.where(strict[d], _mm(a2m, bb, P_CORE, _NT), 0.0)
        sc_k = _mm(jnp.concatenate([a2m, r2m], axis=0), kk2, P_SIDE, _NT)
        l_ak = jnp.where(strict[d], sc_k[0:n2], 0.0)
        m_rk = jnp.where(incl[d], sc_k[n2:2 * n2], 0.0)
        m_rb = jnp.where(incl[d], _mm(r2m, bb, P_SIDE, _NT), 0.0)

        t_m = jnp.where(eye, 1.0, jnp.where(merge_masks[0], l_ab, 0.0))
        for mk in merge_masks[1:]:
            l_off = jnp.where(mk, l_ab, 0.0)
            t_m = t_m + _mm(t_m, _mm(l_off, t_m, P_CORE), P_CORE)

        v2 = stack(v)
        nv = _mm(jnp.concatenate([l_ak, m_rk], axis=0), v2, P_SIDE)
        w1 = _mm(t_m, stack(a_t), P_CORE)
        u0 = _mm(t_m, nv[0:n2], P_SIDE)
        q2 = stack(r_t) + _mm(m_rb, w1, P_SIDE)
        y2 = nv[n2:2 * n2] + _mm(m_rb, u0, P_SIDE)
        q = q2[0:CHUNK] + q2[CHUNK:n2]
        y0 = y2[0:CHUNK] + y2[CHUNK:n2]
        be2 = stack(b_e)
        ke2 = stack(k_e)
        g_l = _mm(w1, be2, P_CORE, _TN)
        h_t = _mm(jnp.concatenate([u0, v2], axis=0),
                  jnp.concatenate([be2, ke2], axis=0), P_SIDE, _TN)

        z = zt_ref[d, p]
        y_ref[0, rows, ls] = _mm(q, z, P_SIDE, _NT) + y0
        zt_ref[d, p] = z * g_c + _mm(z, g_l, P_CORE) + h_t

    n_chunks = TOK_TILE // CHUNK

    def body(i, carry):
        for p in range(HEADS // 2):
            unit(0, p, pl.multiple_of(i * CHUNK, CHUNK), yf_ref)
            unit(1, p, pl.multiple_of((n_chunks - 1 - i) * CHUNK, CHUNK), yb_ref)
        return carry

    lax.fori_loop(0, n_chunks, body, 0)


def _wkv(p_all, wlo, alo, vecs, eh, esel, tri):
    def bwd_tile(j):
        return jnp.where(j == 0, 0, N_TILES - j)

    def bwd_out(j):
        return jnp.where(j == 0, N_LAT_TILES - 1, N_LAT_TILES - j)

    y_shape = jax.ShapeDtypeStruct((BATCH, SEQ, RWKV_WIDTH), F32)
    bs_shape = jax.ShapeDtypeStruct((BATCH, SEQ, LANES), F32)
    return pl.pallas_call(
        _wkv_kernel,
        out_shape=(y_shape, y_shape, bs_shape, bs_shape),
        grid=(BATCH, N_TILES),
        in_specs=[
            pl.BlockSpec((1, TOK_TILE, RWKV_COLS), lambda b, j: (b, j, 0)),
            pl.BlockSpec((1, TOK_TILE, RWKV_COLS), lambda b, j: (b, bwd_tile(j), 0)),
            _const_spec((2, LANES, RWKV_WIDTH)),
            _const_spec((2, LANES, RWKV_WIDTH)),
            _const_spec((SUBLANES, RWKV_WIDTH)),
            _const_spec((RWKV_WIDTH, RWKV_WIDTH)),
            _const_spec((RWKV_WIDTH, LANES)),
            _const_spec((2, TOK_TILE, TOK_TILE)),
        ],
        out_specs=(
            pl.BlockSpec((1, TOK_TILE, RWKV_WIDTH), lambda b, j: (b, jnp.maximum(j - 1, 0), 0)),
            pl.BlockSpec((1, TOK_TILE, RWKV_WIDTH), lambda b, j: (b, bwd_out(j), 0)),
            pl.BlockSpec((1, TOK_TILE, LANES), lambda b, j: (b, jnp.maximum(j - 1, 0), 0)),
            pl.BlockSpec((1, TOK_TILE, LANES), lambda b, j: (b, bwd_out(j), 0)),
        ),
        scratch_shapes=[
            pltpu.VMEM((2, HEADS // 2, LANES, LANES), F32),
            pltpu.VMEM((2 * _N_PLANES, TOK_TILE, RWKV_WIDTH), F32),
        ],
        compiler_params=pltpu.CompilerParams(
            dimension_semantics=("arbitrary", "arbitrary"), vmem_limit_bytes=VMEM_LIMIT),
        name="wkv7_chunked",
    )(p_all, p_all, wlo, alo, vecs, eh, esel, tri)


FOURIER_M_TILE = 512


def _fourier_kernel(x_ref, cd_ref, ab_ref, o_ref, rhs_ref):
    @pl.when(pl.program_id(1) == 0)
    def _():
        for g in range(FOURIER_GROUPS):
            gs = slice(g * FOURIER_GROUP_DIM, (g + 1) * FOURIER_GROUP_DIM)
            z = _dot(x_ref[0, :, gs], cd_ref[...])
            rhs_ref[0:SEQ, gs] = z[:, 0:FOURIER_GROUP_DIM].astype(BF16)
            rhs_ref[SEQ:2 * SEQ, gs] = z[:, FOURIER_GROUP_DIM:2 * FOURIER_GROUP_DIM].astype(BF16)

    o_ref[0] = _dot(ab_ref[...], rhs_ref[...]).astype(BF16)


def _fourier(xf, cd, ab):
    return pl.pallas_call(
        _fourier_kernel,
        out_shape=jax.ShapeDtypeStruct((BATCH, SEQ, FOURIER_WIDTH), BF16),
        grid=(BATCH, SEQ // FOURIER_M_TILE),
        in_specs=[
            pl.BlockSpec((1, SEQ, FOURIER_WIDTH), lambda b, m: (b, 0, 0)),
            _const_spec((FOURIER_GROUP_DIM, 2 * FOURIER_GROUP_DIM)),
            pl.BlockSpec((FOURIER_M_TILE, 2 * SEQ), lambda b, m: (m, 0)),
        ],
        out_specs=pl.BlockSpec((1, FOURIER_M_TILE, FOURIER_WIDTH), lambda b, m: (b, m, 0)),
        scratch_shapes=[pltpu.VMEM((2 * SEQ, FOURIER_WIDTH), BF16)],
        compiler_params=pltpu.CompilerParams(
            dimension_semantics=("arbitrary", "arbitrary"), vmem_limit_bytes=VMEM_LIMIT),
        name="fourier_dft",
    )(xf, cd, ab)


def _merge_kernel(x_ref, mod_ref, yf_ref, yb_ref, bsf_ref, bsb_ref, v_ref, gd_ref, fo_ref,
                  g1_ref, wg_ref, g2_ref, wur_ref, wuf_ref, wo_ref, ehm_ref, ex_ref, lng_ref, lnb_ref, o_ref):
    x = x_ref[0]
    shift = mod_ref[0, :, 0:D_MODEL]
    scale = mod_ref[0, :, D_MODEL:2 * D_MODEL]
    gate1 = mod_ref[0, :, 2 * D_MODEL:3 * D_MODEL]
    hx = ((_rms(x) * g1_ref[...]) * (1.0 + scale) + shift).astype(BF16)
    gates = _dot(hx, wg_ref[...])

    y = yf_ref[0] + yb_ref[0]
    mean = _dot_hilo(y, ehm_ref[...])
    dy = y - mean
    var = _dot_hilo(dy * dy, ehm_ref[...])
    o = dy * lax.rsqrt(var + GN_EPS) * lng_ref[...] + lnb_ref[...]
    bonus = _dot_hilo(bsf_ref[0] + bsb_ref[0], ex_ref[...]) * v_ref[0]
    g = _dot(_sigmoid(gd_ref[0]).astype(BF16), g2_ref[...])
    o = ((o + bonus) * g).astype(BF16)
    r_up = _dot(o, wur_ref[...])
    f_up = _dot(fo_ref[0], wuf_ref[...])
    mix = (_sigmoid(gates[:, 0:D_MODEL]) * f_up + _sigmoid(gates[:, D_MODEL:2 * D_MODEL]) * r_up).astype(BF16)
    o_ref[0] = x + gate1 * _dot(mix, wo_ref[...])


def _merge(x, mod3, yf, yb, bsf, bsb, p_all, fo, g1, wg, g2, wur, wuf, wo, ehm, ex, lng, lnb):
    tok = lambda w: pl.BlockSpec((1, TOK_TILE, w), lambda b, t: (b, t, 0))
    return pl.pallas_call(
        _merge_kernel,
        out_shape=jax.ShapeDtypeStruct((BATCH, SEQ, D_MODEL), F32),
        grid=(BATCH, N_LAT_TILES),
        in_specs=[
            tok(D_MODEL),
            pl.BlockSpec((1, 1, 6 * D_MODEL), lambda b, t: (b, 0, 0)),
            tok(RWKV_WIDTH), tok(RWKV_WIDTH), tok(LANES), tok(LANES),
            pl.BlockSpec((1, TOK_TILE, RWKV_WIDTH), lambda b, t: (b, t + 1, COL_V // RWKV_WIDTH)),
            pl.BlockSpec((1, TOK_TILE, GATE_LORA), lambda b, t: (b, t + 1, COL_GD // GATE_LORA)),
            tok(FOURIER_WIDTH),
            _const_spec((1, D_MODEL)),
            _const_spec((D_MODEL, 2 * D_MODEL)),
            _const_spec((GATE_LORA, RWKV_WIDTH)),
            _const_spec((RWKV_WIDTH, D_MODEL)),
            _const_spec((FOURIER_WIDTH, D_MODEL)),
            _const_spec((D_MODEL, D_MODEL)),
            _const_spec((RWKV_WIDTH, RWKV_WIDTH)),
            _const_spec((LANES, RWKV_WIDTH)),
            _const_spec((1, RWKV_WIDTH)),
            _const_spec((1, RWKV_WIDTH)),
        ],
        out_specs=tok(D_MODEL),
        compiler_params=pltpu.CompilerParams(
            dimension_semantics=("arbitrary", "arbitrary"), vmem_limit_bytes=VMEM_LIMIT),
        name="branch_merge",
    )(x, mod3, yf, yb, bsf, bsb, p_all, p_all, fo, g1, wg, g2, wur, wuf, wo, ehm, ex, lng, lnb)


def _ffn_kernel(x_ref, mod_ref, g2_ref, wgu_ref, wd_ref, gf_ref, o_ref):
    x = x_ref[0]
    shift = mod_ref[0, :, 3 * D_MODEL:4 * D_MODEL]
    scale = mod_ref[0, :, 4 * D_MODEL:5 * D_MODEL]
    gate2 = mod_ref[0, :, 5 * D_MODEL:6 * D_MODEL]
    hx = ((_rms(x) * g2_ref[...]) * (1.0 + scale) + shift).astype(BF16)
    part = D_FF // FFN_SPLIT
    acc = jnp.zeros((FFN_TILE, D_MODEL), F32)
    for s in range(FFN_SPLIT):
        gt = _dot(hx, wgu_ref[:, s * part:(s + 1) * part])
        up = _dot(hx, wgu_ref[:, D_FF + s * part:D_FF + (s + 1) * part])
        h = (gt * _sigmoid(gt) * up).astype(BF16)
        acc = acc + _dot(h, wd_ref[s * part:(s + 1) * part, :])
    o_ref[0] = _rms(x + gate2 * acc) * gf_ref[...]


def _ffn(x1, mod3, g2, wgu, wd, gf):
    tiles = SEQ // FFN_TILE
    return pl.pallas_call(
        _ffn_kernel,
        out_shape=jax.ShapeDtypeStruct((BATCH, SEQ, D_MODEL), F32),
        grid=(BATCH, tiles),
        in_specs=[
            pl.BlockSpec((1, FFN_TILE, D_MODEL), lambda b, t: (b, t, 0)),
            pl.BlockSpec((1, 1, 6 * D_MODEL), lambda b, t: (b, 0, 0)),
            _const_spec((1, D_MODEL)),
            _const_spec((D_MODEL, 2 * D_FF)),
            _const_spec((D_FF, D_MODEL)),
            _const_spec((1, D_MODEL)),
        ],
        out_specs=pl.BlockSpec((1, FFN_TILE, D_MODEL), lambda b, t: (b, t, 0)),
        compiler_params=pltpu.CompilerParams(
            dimension_semantics=("arbitrary", "arbitrary"), vmem_limit_bytes=VMEM_LIMIT),
        name="swiglu_final",
    )(x1, mod3, g2, wgu, wd, gf)


@functools.lru_cache(maxsize=None)
def _constants():
    rows = SEQ // GRID_W
    t = np.arange(SEQ)
    tr, tc = t // GRID_W, t % GRID_W
    num = (np.outer(tr, tr) * (GRID_W // rows) + np.outer(tc, tc)) % GRID_W
    ang = 2.0 * np.pi * num / GRID_W
    ab = np.concatenate([np.cos(ang), np.sin(ang)], axis=1)
    jj = np.outer(np.arange(FOURIER_GROUP_DIM), np.arange(FOURIER_GROUP_DIM)) % FOURIER_GROUP_DIM
    ang_c = 2.0 * np.pi * jj / FOURIER_GROUP_DIM
    norm = 1.0 / np.sqrt(rows * GRID_W * FOURIER_GROUP_DIM)
    cd = np.concatenate([np.cos(ang_c), -np.sin(ang_c)], axis=1) * norm
    head = np.arange(RWKV_WIDTH) // HEAD_DIM
    eh = (head[:, None] == head[None, :]).astype(np.float32)
    esel = (head[:, None] == np.arange(LANES)[None, :]).astype(np.float32)
    i = np.arange(TOK_TILE)
    same_chunk = (i[:, None] // CHUNK) == (i[None, :] // CHUNK)
    tri = np.stack([same_chunk & (i[None, :] <= i[:, None]), same_chunk & (i[None, :] >= i[:, None])])
    f32 = lambda u: np.asarray(u, np.float32)
    return dict(ab=f32(ab), cd=f32(cd), eh=f32(eh), ehm=f32(eh / HEAD_DIM), esel=f32(esel), ex=f32(esel.T),
                tri=f32(tri))


def kernel(x, c, ctx, c_ctx, norm1_g, norm2_g, w_ada, b_ada, w_in, mu_prev, mu_next, w0_f, w2_f, a0_f, a2_f, w0_b, w2_b, a0_b, a2_b, g2, k_k, k_a, r_k, lnx_g, lnx_b, w_up_r, w_up_f, w_out, w_gu, w_down, final_norm_g):
    cst = {name: jnp.asarray(val) if name == "tri" else jnp.asarray(val).astype(BF16)
           for name, val in _constants().items()}
    row = lambda u: u.reshape(1, -1)
    cc = jnp.concatenate(
        [c, c_ctx[None, :], jnp.zeros((MOD_ROWS - BATCH - 1, D_MODEL), F32)], axis=0)
    mod = _modulation(cc, w_ada[0].astype(BF16), row(b_ada[0]))
    mod3 = mod.reshape(MOD_ROWS, 1, 6 * D_MODEL)

    w_in0 = w_in[0].astype(BF16)
    p_all, xf = _inproj(x, ctx, mod3, row(norm1_g[0]), w_in0[:, 0:GATE_START], row(mu_prev[0]), row(mu_next[0]))

    zeros_lora = jnp.zeros((DECAY_LORA, RWKV_WIDTH), F32)
    wlo = jnp.stack([jnp.concatenate([w2_f[0], zeros_lora], 0), jnp.concatenate([zeros_lora, w2_b[0]], 0)]).astype(BF16)
    alo = jnp.stack([jnp.concatenate([a2_f[0], zeros_lora], 0), jnp.concatenate([zeros_lora, a2_b[0]], 0)]).astype(BF16)
    vecs = jnp.stack([w0_f[0], w0_b[0], a0_f[0], a0_b[0], k_k[0], k_a[0], r_k[0].reshape(-1),
                      jnp.zeros((RWKV_WIDTH,), F32)])
    yf, yb, bsf, bsb = _wkv(p_all, wlo, alo, vecs, cst["eh"], cst["esel"], cst["tri"])

    fo = _fourier(xf, cst["cd"], cst["ab"])

    x1 = _merge(x, mod3, yf, yb, bsf, bsb, p_all, fo, row(norm1_g[0]), w_in0[:, GATE_START:],
                g2[0].astype(BF16), w_up_r[0].astype(BF16), w_up_f[0].astype(BF16), w_out[0].astype(BF16),
                cst["ehm"], cst["ex"], row(lnx_g[0]), row(lnx_b[0]))
    return _ffn(x1, mod3, row(norm2_g[0]), w_gu[0].astype(BF16), w_down[0].astype(BF16), row(final_norm_g))
```

```python
import functools

import jax
import jax.numpy as jnp
import numpy as np
from jax import lax
from jax.experimental import pallas as pl
from jax.experimental.pallas import tpu as pltpu

D_MODEL = 1024
BATCH = 16
SEQ = 2048
GRID_W = 64
CTX_LEN = 256
FOURIER_WIDTH = 512
FOURIER_GROUPS = 4
FOURIER_GROUP_DIM = FOURIER_WIDTH // FOURIER_GROUPS
RWKV_WIDTH = 512
HEAD_DIM = 64
HEADS = RWKV_WIDTH // HEAD_DIM
DECAY_LORA = 64
AAA_LORA = 64
GATE_LORA = 128
RWKV_COLS = 3 * RWKV_WIDTH + 2 * DECAY_LORA + 2 * AAA_LORA + GATE_LORA
FOURIER_START = RWKV_COLS
GATE_START = RWKV_COLS + FOURIER_WIDTH
D_FF = 2816
NORM_EPS = 1e-6
GN_EPS = 64e-5

COL_R, COL_K, COL_V = 0, RWKV_WIDTH, 2 * RWKV_WIDTH
COL_WD = 3 * RWKV_WIDTH
COL_AD = COL_WD + 2 * DECAY_LORA
COL_GD = COL_AD + 2 * AAA_LORA

LANES = 128
SUBLANES = 8
TOK_TILE = 256
CHUNK = 64
SEQ_ALL = CTX_LEN + SEQ
N_TILES = SEQ_ALL // TOK_TILE
N_LAT_TILES = SEQ // TOK_TILE
FFN_TILE = 512
FFN_SPLIT = 2
MOD_ROWS = 24
VMEM_LIMIT = 56 * 1024 * 1024

F32 = jnp.float32
BF16 = jnp.bfloat16


def _dot(a, b):
    return jnp.dot(a, b, preferred_element_type=F32)


def _dot_nt(a, b):
    return lax.dot_general(a, b, (((1,), (1,)), ((), ())), preferred_element_type=F32)


def _dot_tn(a, b):
    return lax.dot_general(a, b, (((0,), (0,)), ((), ())), preferred_element_type=F32)


def _dot_hilo(x, e):
    hi = x.astype(BF16)
    lo = (x - hi.astype(F32)).astype(BF16)
    return _dot(hi, e) + _dot(lo, e)


def _dot_split3(e, x):
    x1 = x.astype(BF16)
    rem = x - x1.astype(F32)
    x2 = rem.astype(BF16)
    x3 = (rem - x2.astype(F32)).astype(BF16)
    return _dot(e, x1) + _dot(e, x2) + _dot(e, x3)


def _rms(u):
    return u * lax.rsqrt(jnp.mean(u * u, axis=-1, keepdims=True) + NORM_EPS)


def _sigmoid(z):
    return 1.0 / (1.0 + jnp.exp(-z))


def _softplus(z):
    return jnp.maximum(z, 0.0) + jnp.log(1.0 + jnp.exp(-jnp.abs(z)))


def _const_spec(shape):
    nd = len(shape)
    return pl.BlockSpec(shape, lambda *_: (0,) * nd, pipeline_mode=pl.Buffered(1))


def _mod_kernel(c_ref, w_ref, b_ref, o_ref):
    c = c_ref[...]
    s = (c * _sigmoid(c)).astype(BF16)
    o_ref[...] = _dot(s, w_ref[...]) + b_ref[...]


def _modulation(cc, w_ada, b_ada):
    n_blk = 4
    bn = (6 * D_MODEL) // n_blk
    return pl.pallas_call(
        _mod_kernel,
        out_shape=jax.ShapeDtypeStruct((MOD_ROWS, 6 * D_MODEL), F32),
        grid=(n_blk,),
        in_specs=[
            pl.BlockSpec((MOD_ROWS, D_MODEL), lambda n: (0, 0)),
            pl.BlockSpec((D_MODEL, bn), lambda n: (0, n)),
            pl.BlockSpec((1, bn), lambda n: (0, n)),
        ],
        out_specs=pl.BlockSpec((MOD_ROWS, bn), lambda n: (0, n)),
        name="adaln_mod",
    )(cc, w_ada, b_ada)


def _inproj_kernel(x_ref, xp_ref, xn_ref, ctx_ref, modb_ref, modc_ref, g1_ref, w_ref, mup_ref, mun_ref,
                   p_ref, xf_ref, pe_ref):
    j = pl.program_id(1)
    is_ctx = j == 0
    g1 = g1_ref[...]
    shift = jnp.where(is_ctx, modc_ref[0, :, 0:D_MODEL], modb_ref[0, :, 0:D_MODEL])
    scale = jnp.where(is_ctx, modc_ref[0, :, D_MODEL:2 * D_MODEL], modb_ref[0, :, D_MODEL:2 * D_MODEL])

    def norm_mod(u):
        return (_rms(u) * g1) * (1.0 + scale) + shift

    main = jnp.where(is_ctx, ctx_ref[0], x_ref[0])
    prev_ok = jnp.where(j >= 2, 1.0, 0.0)
    next_ok = jnp.where(jnp.logical_and(j >= 1, j <= N_LAT_TILES - 1), 1.0, 0.0)
    h_ext = jnp.concatenate(
        [norm_mod(xp_ref[0]) * prev_ok, norm_mod(main), norm_mod(xn_ref[0]) * next_ok], axis=0).astype(BF16)
    p_ext = _dot(h_ext, w_ref[...])
    xf_ref[0] = p_ext[SUBLANES:SUBLANES + TOK_TILE, FOURIER_START:GATE_START].astype(BF16)
    pe_ref[...] = p_ext[:, 0:RWKV_COLS]
    pm = pe_ref[pl.ds(SUBLANES, TOK_TILE), :]
    pu = pe_ref[pl.ds(SUBLANES - 1, TOK_TILE), :]
    pd = pe_ref[pl.ds(SUBLANES + 1, TOK_TILE), :]
    p_ref[0] = pm + mup_ref[...] * (pu - pm) + mun_ref[...] * (pd - pm)


def _inproj(x, ctx, mod3, g1, w_rf, mu_prev, mu_next):
    halo_blocks = TOK_TILE // SUBLANES
    last_halo = SEQ // SUBLANES - 1
    return pl.pallas_call(
        _inproj_kernel,
        out_shape=(jax.ShapeDtypeStruct((BATCH, SEQ_ALL, RWKV_COLS), F32),
                   jax.ShapeDtypeStruct((BATCH, SEQ, FOURIER_WIDTH), BF16)),
        grid=(BATCH, N_TILES),
        in_specs=[
            pl.BlockSpec((1, TOK_TILE, D_MODEL), lambda b, j: (b, jnp.maximum(j - 1, 0), 0)),
            pl.BlockSpec((1, SUBLANES, D_MODEL),
                         lambda b, j: (b, jnp.clip((j - 1) * halo_blocks - 1, 0, last_halo), 0)),
            pl.BlockSpec((1, SUBLANES, D_MODEL),
                         lambda b, j: (b, jnp.clip(j * halo_blocks, 0, last_halo), 0)),
            pl.BlockSpec((1, CTX_LEN, D_MODEL), lambda b, j: (b, 0, 0)),
            pl.BlockSpec((1, 1, 2 * D_MODEL), lambda b, j: (b, 0, 0)),
            pl.BlockSpec((1, 1, 2 * D_MODEL), lambda b, j: (BATCH, 0, 0)),
            _const_spec((1, D_MODEL)),
            _const_spec((D_MODEL, GATE_START)),
            _const_spec((1, RWKV_COLS)),
            _const_spec((1, RWKV_COLS)),
        ],
        out_specs=(
            pl.BlockSpec((1, TOK_TILE, RWKV_COLS), lambda b, j: (b, j, 0)),
            pl.BlockSpec((1, TOK_TILE, FOURIER_WIDTH), lambda b, j: (b, jnp.maximum(j - 1, 0), 0)),
        ),
        scratch_shapes=[pltpu.VMEM((TOK_TILE + 2 * SUBLANES, RWKV_COLS), F32)],
        compiler_params=pltpu.CompilerParams(
            dimension_semantics=("arbitrary", "arbitrary"), vmem_limit_bytes=VMEM_LIMIT),
        name="inproj_shift",
    )(x, x, x, ctx, mod3, mod3, g1, w_rf, mu_prev, mu_next)


_S_R, _S_V, _S_KD, _S_AL, _S_BE, _S_LD, _S_LGI = range(7)
_N_PLANES = 7
_V_W0F, _V_W0B, _V_A0F, _V_A0B, _V_KK, _V_KA, _V_RK = range(7)


def _wkv_kernel(pf_ref, pb_ref, wlo_ref, alo_ref, vec_ref, eh_ref, esel_ref, tri_ref,
                yf_ref, yb_ref, bsf_ref, bsb_ref, zt_ref, s_ref):
    j = pl.program_id(1)

    @pl.when(j == 0)
    def _():
        zt_ref[...] = jnp.zeros_like(zt_ref)

    k_k = vec_ref[_V_KK:_V_KK + 1, :]
    k_a = vec_ref[_V_KA:_V_KA + 1, :]
    r_k = vec_ref[_V_RK:_V_RK + 1, :]

    for d, (p_ref, bs_ref) in enumerate(((pf_ref, bsf_ref), (pb_ref, bsb_ref))):
        w0 = vec_ref[_V_W0F + d:_V_W0F + d + 1, :]
        a0 = vec_ref[_V_A0F + d:_V_A0F + d + 1, :]
        r = p_ref[0, :, COL_R:COL_R + RWKV_WIDTH]
        k = p_ref[0, :, COL_K:COL_K + RWKV_WIDTH]
        v = p_ref[0, :, COL_V:COL_V + RWKV_WIDTH]
        wa = p_ref[0, :, COL_WD:COL_WD + LANES]
        aa = p_ref[0, :, COL_AD:COL_AD + LANES]
        kq = k * k_k
        ss = _dot((kq * kq).astype(BF16), eh_ref[...])
        kk = kq * lax.rsqrt(jnp.maximum(ss, 1e-24))
        wl = w0 + _dot(jnp.tanh(wa).astype(BF16), wlo_ref[d])
        ld = -jnp.exp(-_softplus(-wl) - 0.5)
        a = _sigmoid(a0 + _dot(aa.astype(BF16), alo_ref[d]))
        kd = k * (1.0 + (a - 1.0) * k_a)
        base = d * _N_PLANES
        s_ref[base + _S_R] = r
        s_ref[base + _S_V] = v
        s_ref[base + _S_KD] = kd
        s_ref[base + _S_AL] = -kk
        s_ref[base + _S_BE] = kk * a
        s_ref[base + _S_LD] = ld
        s_ref[base + _S_LGI] = _dot_split3(tri_ref[d], ld)
        bs_ref[0] = _dot((r * kd * r_k).astype(BF16), esel_ref[...])

    ri = lax.broadcasted_iota(jnp.int32, (CHUNK, LANES), 0)
    li = lax.broadcasted_iota(jnp.int32, (CHUNK, LANES), 1)
    ci = jnp.bitwise_and(li, HEAD_DIM - 1)
    lane_lo = li < HEAD_DIM
    eye = ri == ci
    strict = (ci < ri, ci > ri)
    incl = (ci <= ri, ci >= ri)
    merge_masks = [jnp.right_shift(ri, 1) == jnp.right_shift(ci, 1)]
    for sh in range(1, CHUNK.bit_length() - 1):
        merge_masks.append(jnp.logical_and(jnp.right_shift(ri, sh + 1) == jnp.right_shift(ci, sh + 1),
                                           jnp.right_shift(ri, sh) != jnp.right_shift(ci, sh)))
    n2 = 2 * CHUNK
    same_head = (lax.broadcasted_iota(jnp.int32, (n2, LANES), 0) >= CHUNK) == \
        (lax.broadcasted_iota(jnp.int32, (n2, LANES), 1) >= HEAD_DIM)

    def stack(u):
        return jnp.concatenate([jnp.where(lane_lo, u, 0.0), jnp.where(lane_lo, 0.0, u)], axis=0)

    def load(d, p, r0):
        base = d * _N_PLANES
        ls = slice(p * LANES, (p + 1) * LANES)
        rows = slice(r0, r0 + CHUNK)
        return tuple(s_ref[base + pln, rows, ls] for pln in (_S_LD, _S_LGI, _S_R, _S_V, _S_KD, _S_AL, _S_BE))

    def phase_a(dirs, ins):
        us = range(len(dirs))
        ld, lgi, r, v, kd, al, be = ([ins[u][f] for u in us] for f in range(7))
        m = [lgi[u][CHUNK // 2:CHUNK // 2 + 1, :] for u in us]
        last = [CHUNK - 1 if dirs[u] == 0 else 0 for u in us]
        lgc = [lgi[u][last[u]:last[u] + 1, :] for u in us]
        lge = [lgi[u] - ld[u] for u in us]
        e_m = [jnp.exp(m[u] - lgi[u]) for u in us]
        lhs = [jnp.concatenate([al[u] * jnp.exp(lge[u] - m[u]), r[u] * jnp.exp(lgi[u] - m[u])], axis=0).astype(BF16)
               for u in us]
        rhs = [jnp.concatenate([stack(be[u] * e_m[u]), stack(kd[u] * e_m[u])], axis=0).astype(BF16) for u in us]
        sc = [_dot_nt(lhs[u], rhs[u]) for u in us]
        l_ab = [jnp.where(strict[dirs[u]], sc[u][0:CHUNK, 0:LANES], 0.0) for u in us]

        t_m = [jnp.where(eye, 1.0, jnp.where(merge_masks[0], l_ab[u], 0.0)) for u in us]
        for mk in merge_masks[1:]:
            t_bd = [stack(t_m[u]).astype(BF16) for u in us]
            e_l = [_dot(jnp.where(mk, l_ab[u], 0.0).astype(BF16), t_bd[u]) for u in us]
            t_m = [t_m[u] + _dot(t_m[u].astype(BF16), stack(e_l[u]).astype(BF16)) for u in us]
        t_b = [t_m[u].astype(BF16) for u in us]

        v2 = [stack(v[u]).astype(BF16) for u in us]
        nv = [_dot(jnp.concatenate([jnp.where(strict[dirs[u]], sc[u][0:CHUNK, LANES:2 * LANES], 0.0),
                                    jnp.where(incl[dirs[u]], sc[u][CHUNK:n2, LANES:2 * LANES], 0.0)],
                                   axis=0).astype(BF16), v2[u]) for u in us]
        wu = [_dot(t_b[u], jnp.concatenate([stack(al[u] * jnp.exp(lge[u])), stack(nv[u][0:CHUNK])],
                                           axis=1).astype(BF16)) for u in us]
        m_rb = [jnp.where(incl[dirs[u]], sc[u][CHUNK:n2, 0:LANES], 0.0).astype(BF16) for u in us]
        qy = [_dot(m_rb[u], jnp.concatenate([stack(wu[u][:, 0:LANES]), stack(wu[u][:, LANES:2 * LANES])],
                                            axis=1).astype(BF16)) for u in us]
        q = [(r[u] * jnp.exp(lgi[u]) + qy[u][:, 0:LANES]).astype(BF16) for u in us]
        y0 = [nv[u][CHUNK:n2] + qy[u][:, LANES:2 * LANES] for u in us]
        e_c = [jnp.exp(lgc[u] - lgi[u]) for u in us]
        b_e = [(be[u] * e_c[u]).astype(BF16) for u in us]
        g_l = [jnp.where(same_head, _dot_tn(wu[u][:, 0:LANES].astype(BF16), b_e[u]), 0.0).astype(BF16)
               for u in us]
        h_f = [_dot_tn(jnp.concatenate([wu[u][:, LANES:2 * LANES], v[u]], axis=0).astype(BF16),
                       jnp.concatenate([b_e[u], (kd[u] * e_c[u]).astype(BF16)], axis=0)) for u in us]
        h_t = [jnp.where(lane_lo, h_f[u][0:CHUNK], h_f[u][CHUNK:n2]) for u in us]
        g_c = [jnp.exp(lgc[u]) for u in us]
        return q, y0, g_l, h_t, g_c

    n_chunks = TOK_TILE // CHUNK
    pairs = HEADS // 2
    units = [(d, p, i if d == 0 else n_chunks - 1 - i)
             for i in range(n_chunks) for d in range(2) for p in range(pairs)]
    ins = [load(d, p, c * CHUNK) for d, p, c in units]
    q, y0, g_l, h_t, g_c = phase_a([d for d, _, _ in units], ins)
    y_refs = (yf_ref, yb_ref)
    z = {(d, p): zt_ref[d, p] for d in range(2) for p in range(pairs)}
    for u, (d, p, c) in enumerate(units):
        y_refs[d][0, c * CHUNK:(c + 1) * CHUNK, p * LANES:(p + 1) * LANES] = \
            _dot_nt(q[u], stack(z[d, p]).astype(BF16)) + y0[u]
        z[d, p] = z[d, p] * g_c[u] + _dot(z[d, p].astype(BF16), g_l[u]) + h_t[u]
    for (d, p), val in z.items():
        zt_ref[d, p] = val


def _wkv(p_all, wlo, alo, vecs, eh, esel, tri):
    def bwd_tile(j):
        return jnp.where(j == 0, 0, N_TILES - j)

    def bwd_out(j):
        return jnp.where(j == 0, N_LAT_TILES - 1, N_LAT_TILES - j)

    y_shape = jax.ShapeDtypeStruct((BATCH, SEQ, RWKV_WIDTH), F32)
    bs_shape = jax.ShapeDtypeStruct((BATCH, SEQ, LANES), F32)
    return pl.pallas_call(
        _wkv_kernel,
        out_shape=(y_shape, y_shape, bs_shape, bs_shape),
        grid=(BATCH, N_TILES),
        in_specs=[
            pl.BlockSpec((1, TOK_TILE, RWKV_COLS), lambda b, j: (b, j, 0)),
            pl.BlockSpec((1, TOK_TILE, RWKV_COLS), lambda b, j: (b, bwd_tile(j), 0)),
            _const_spec((2, LANES, RWKV_WIDTH)),
            _const_spec((2, LANES, RWKV_WIDTH)),
            _const_spec((SUBLANES, RWKV_WIDTH)),
            _const_spec((RWKV_WIDTH, RWKV_WIDTH)),
            _const_spec((RWKV_WIDTH, LANES)),
            _const_spec((2, TOK_TILE, TOK_TILE)),
        ],
        out_specs=(
            pl.BlockSpec((1, TOK_TILE, RWKV_WIDTH), lambda b, j: (b, jnp.maximum(j - 1, 0), 0)),
            pl.BlockSpec((1, TOK_TILE, RWKV_WIDTH), lambda b, j: (b, bwd_out(j), 0)),
            pl.BlockSpec((1, TOK_TILE, LANES), lambda b, j: (b, jnp.maximum(j - 1, 0), 0)),
            pl.BlockSpec((1, TOK_TILE, LANES), lambda b, j: (b, bwd_out(j), 0)),
        ),
        scratch_shapes=[
            pltpu.VMEM((2, HEADS // 2, HEAD_DIM, LANES), F32),
            pltpu.VMEM((2 * _N_PLANES, TOK_TILE, RWKV_WIDTH), F32),
        ],
        compiler_params=pltpu.CompilerParams(
            dimension_semantics=("arbitrary", "arbitrary"), vmem_limit_bytes=VMEM_LIMIT),
        name="wkv7_chunked",
    )(p_all, p_all, wlo, alo, vecs, eh, esel, tri)


FOURIER_M_TILE = 512


def _fourier_kernel(x_ref, cd_ref, ab_ref, o_ref, rhs_ref):
    @pl.when(pl.program_id(1) == 0)
    def _():
        for g in range(FOURIER_GROUPS):
            gs = slice(g * FOURIER_GROUP_DIM, (g + 1) * FOURIER_GROUP_DIM)
            z = _dot(x_ref[0, :, gs], cd_ref[...])
            rhs_ref[0:SEQ, gs] = z[:, 0:FOURIER_GROUP_DIM].astype(BF16)
            rhs_ref[SEQ:2 * SEQ, gs] = z[:, FOURIER_GROUP_DIM:2 * FOURIER_GROUP_DIM].astype(BF16)

    o_ref[0] = _dot(ab_ref[...], rhs_ref[...]).astype(BF16)


def _fourier(xf, cd, ab):
    return pl.pallas_call(
        _fourier_kernel,
        out_shape=jax.ShapeDtypeStruct((BATCH, SEQ, FOURIER_WIDTH), BF16),
        grid=(BATCH, SEQ // FOURIER_M_TILE),
        in_specs=[
            pl.BlockSpec((1, SEQ, FOURIER_WIDTH), lambda b, m: (b, 0, 0)),
            _const_spec((FOURIER_GROUP_DIM, 2 * FOURIER_GROUP_DIM)),
            pl.BlockSpec((FOURIER_M_TILE, 2 * SEQ), lambda b, m: (m, 0)),
        ],
        out_specs=pl.BlockSpec((1, FOURIER_M_TILE, FOURIER_WIDTH), lambda b, m: (b, m, 0)),
        scratch_shapes=[pltpu.VMEM((2 * SEQ, FOURIER_WIDTH), BF16)],
        compiler_params=pltpu.CompilerParams(
            dimension_semantics=("arbitrary", "arbitrary"), vmem_limit_bytes=VMEM_LIMIT),
        name="fourier_dft",
    )(xf, cd, ab)


def _merge_kernel(x_ref, mod_ref, yf_ref, yb_ref, bsf_ref, bsb_ref, v_ref, gd_ref, fo_ref,
                  g1_ref, wg_ref, g2_ref, wur_ref, wuf_ref, wo_ref, ehm_ref, ex_ref, lng_ref, lnb_ref, o_ref):
    x = x_ref[0]
    shift = mod_ref[0, :, 0:D_MODEL]
    scale = mod_ref[0, :, D_MODEL:2 * D_MODEL]
    gate1 = mod_ref[0, :, 2 * D_MODEL:3 * D_MODEL]
    hx = ((_rms(x) * g1_ref[...]) * (1.0 + scale) + shift).astype(BF16)
    gates = _dot(hx, wg_ref[...])

    y = yf_ref[0] + yb_ref[0]
    mean = _dot_hilo(y, ehm_ref[...])
    dy = y - mean
    var = _dot_hilo(dy * dy, ehm_ref[...])
    o = dy * lax.rsqrt(var + GN_EPS) * lng_ref[...] + lnb_ref[...]
    bonus = _dot_hilo(bsf_ref[0] + bsb_ref[0], ex_ref[...]) * v_ref[0]
    g = _dot(_sigmoid(gd_ref[0]).astype(BF16), g2_ref[...])
    o = ((o + bonus) * g).astype(BF16)
    r_up = _dot(o, wur_ref[...])
    f_up = _dot(fo_ref[0], wuf_ref[...])
    mix = (_sigmoid(gates[:, 0:D_MODEL]) * f_up + _sigmoid(gates[:, D_MODEL:2 * D_MODEL]) * r_up).astype(BF16)
    o_ref[0] = x + gate1 * _dot(mix, wo_ref[...])


def _merge(x, mod3, yf, yb, bsf, bsb, p_all, fo, g1, wg, g2, wur, wuf, wo, ehm, ex, lng, lnb):
    tok = lambda w: pl.BlockSpec((1, TOK_TILE, w), lambda b, t: (b, t, 0))
    return pl.pallas_call(
        _merge_kernel,
        out_shape=jax.ShapeDtypeStruct((BATCH, SEQ, D_MODEL), F32),
        grid=(BATCH, N_LAT_TILES),
        in_specs=[
            tok(D_MODEL),
            pl.BlockSpec((1, 1, 6 * D_MODEL), lambda b, t: (b, 0, 0)),
            tok(RWKV_WIDTH), tok(RWKV_WIDTH), tok(LANES), tok(LANES),
            pl.BlockSpec((1, TOK_TILE, RWKV_WIDTH), lambda b, t: (b, t + 1, COL_V // RWKV_WIDTH)),
            pl.BlockSpec((1, TOK_TILE, GATE_LORA), lambda b, t: (b, t + 1, COL_GD // GATE_LORA)),
            tok(FOURIER_WIDTH),
            _const_spec((1, D_MODEL)),
            _const_spec((D_MODEL, 2 * D_MODEL)),
            _const_spec((GATE_LORA, RWKV_WIDTH)),
            _const_spec((RWKV_WIDTH, D_MODEL)),
            _const_spec((FOURIER_WIDTH, D_MODEL)),
            _const_spec((D_MODEL, D_MODEL)),
            _const_spec((RWKV_WIDTH, RWKV_WIDTH)),
            _const_spec((LANES, RWKV_WIDTH)),
            _const_spec((1, RWKV_WIDTH)),
            _const_spec((1, RWKV_WIDTH)),
        ],
        out_specs=tok(D_MODEL),
        compiler_params=pltpu.CompilerParams(
            dimension_semantics=("arbitrary", "arbitrary"), vmem_limit_bytes=VMEM_LIMIT),
        name="branch_merge",
    )(x, mod3, yf, yb, bsf, bsb, p_all, p_all, fo, g1, wg, g2, wur, wuf, wo, ehm, ex, lng, lnb)


def _ffn_kernel(x_ref, mod_ref, g2_ref, wgu_ref, wd_ref, gf_ref, o_ref):
    x = x_ref[0]
    shift = mod_ref[0, :, 3 * D_MODEL:4 * D_MODEL]
    scale = mod_ref[0, :, 4 * D_MODEL:5 * D_MODEL]
    gate2 = mod_ref[0, :, 5 * D_MODEL:6 * D_MODEL]
    hx = ((_rms(x) * g2_ref[...]) * (1.0 + scale) + shift).astype(BF16)
    part = D_FF // FFN_SPLIT
    acc = jnp.zeros((FFN_TILE, D_MODEL), F32)
    for s in range(FFN_SPLIT):
        gt = _dot(hx, wgu_ref[:, s * part:(s + 1) * part])
        up = _dot(hx, wgu_ref[:, D_FF + s * part:D_FF + (s + 1) * part])
        h = (gt * _sigmoid(gt) * up).astype(BF16)
        acc = acc + _dot(h, wd_ref[s * part:(s + 1) * part, :])
    o_ref[0] = _rms(x + gate2 * acc) * gf_ref[...]


def _ffn(x1, mod3, g2, wgu, wd, gf):
    tiles = SEQ // FFN_TILE
    return pl.pallas_call(
        _ffn_kernel,
        out_shape=jax.ShapeDtypeStruct((BATCH, SEQ, D_MODEL), F32),
        grid=(BATCH, tiles),
        in_specs=[
            pl.BlockSpec((1, FFN_TILE, D_MODEL), lambda b, t: (b, t, 0)),
            pl.BlockSpec((1, 1, 6 * D_MODEL), lambda b, t: (b, 0, 0)),
            _const_spec((1, D_MODEL)),
            _const_spec((D_MODEL, 2 * D_FF)),
            _const_spec((D_FF, D_MODEL)),
            _const_spec((1, D_MODEL)),
        ],
        out_specs=pl.BlockSpec((1, FFN_TILE, D_MODEL), lambda b, t: (b, t, 0)),
        compiler_params=pltpu.CompilerParams(
            dimension_semantics=("arbitrary", "arbitrary"), vmem_limit_bytes=VMEM_LIMIT),
        name="swiglu_final",
    )(x1, mod3, g2, wgu, wd, gf)


@functools.lru_cache(maxsize=None)
def _constants():
    rows = SEQ // GRID_W
    t = np.arange(SEQ)
    tr, tc = t // GRID_W, t % GRID_W
    num = (np.outer(tr, tr) * (GRID_W // rows) + np.outer(tc, tc)) % GRID_W
    ang = 2.0 * np.pi * num / GRID_W
    ab = np.concatenate([np.cos(ang), np.sin(ang)], axis=1)
    jj = np.outer(np.arange(FOURIER_GROUP_DIM), np.arange(FOURIER_GROUP_DIM)) % FOURIER_GROUP_DIM
    ang_c = 2.0 * np.pi * jj / FOURIER_GROUP_DIM
    norm = 1.0 / np.sqrt(rows * GRID_W * FOURIER_GROUP_DIM)
    cd = np.concatenate([np.cos(ang_c), -np.sin(ang_c)], axis=1) * norm
    head = np.arange(RWKV_WIDTH) // HEAD_DIM
    eh = (head[:, None] == head[None, :]).astype(np.float32)
    esel = (head[:, None] == np.arange(LANES)[None, :]).astype(np.float32)
    i = np.arange(TOK_TILE)
    same_chunk = (i[:, None] // CHUNK) == (i[None, :] // CHUNK)
    tri = np.stack([same_chunk & (i[None, :] <= i[:, None]), same_chunk & (i[None, :] >= i[:, None])])
    f32 = lambda u: np.asarray(u, np.float32)
    return dict(ab=f32(ab), cd=f32(cd), eh=f32(eh), ehm=f32(eh / HEAD_DIM), esel=f32(esel), ex=f32(esel.T),
                tri=f32(tri))


def kernel(x, c, ctx, c_ctx, norm1_g, norm2_g, w_ada, b_ada, w_in, mu_prev, mu_next, w0_f, w2_f, a0_f, a2_f, w0_b, w2_b, a0_b, a2_b, g2, k_k, k_a, r_k, lnx_g, lnx_b, w_up_r, w_up_f, w_out, w_gu, w_down, final_norm_g):
    cst = {name: jnp.asarray(val).astype(BF16) for name, val in _constants().items()}
    row = lambda u: u.reshape(1, -1)
    cc = jnp.concatenate(
        [c, c_ctx[None, :], jnp.zeros((MOD_ROWS - BATCH - 1, D_MODEL), F32)], axis=0)
    mod = _modulation(cc, w_ada[0].astype(BF16), row(b_ada[0]))
    mod3 = mod.reshape(MOD_ROWS, 1, 6 * D_MODEL)

    w_in0 = w_in[0].astype(BF16)
    p_all, xf = _inproj(x, ctx, mod3, row(norm1_g[0]), w_in0[:, 0:GATE_START], row(mu_prev[0]), row(mu_next[0]))

    zeros_lora = jnp.zeros((DECAY_LORA, RWKV_WIDTH), F32)
    wlo = jnp.stack([jnp.concatenate([w2_f[0], zeros_lora], 0), jnp.concatenate([zeros_lora, w2_b[0]], 0)]).astype(BF16)
    alo = jnp.stack([jnp.concatenate([a2_f[0], zeros_lora], 0), jnp.concatenate([zeros_lora, a2_b[0]], 0)]).astype(BF16)
    vecs = jnp.stack([w0_f[0], w0_b[0], a0_f[0], a0_b[0], k_k[0], k_a[0], r_k[0].reshape(-1),
                      jnp.zeros((RWKV_WIDTH,), F32)])
    yf, yb, bsf, bsb = _wkv(p_all, wlo, alo, vecs, cst["eh"], cst["esel"], cst["tri"])

    fo = _fourier(xf, cst["cd"], cst["ab"])

    x1 = _merge(x, mod3, yf, yb, bsf, bsb, p_all, fo, row(norm1_g[0]), w_in0[:, GATE_START:],
                g2[0].astype(BF16), w_up_r[0].astype(BF16), w_up_f[0].astype(BF16), w_out[0].astype(BF16),
                cst["ehm"], cst["ex"], row(lnx_g[0]), row(lnx_b[0]))
    return _ffn(x1, mod3, row(norm2_g[0]), w_gu[0].astype(BF16), w_down[0].astype(BF16), row(final_norm_g))
```

```python
import functools

import jax
import jax.numpy as jnp
import numpy as np
from jax import lax
from jax.experimental import pallas as pl
from jax.experimental.pallas import tpu as pltpu

D_MODEL = 1024
BATCH = 16
SEQ = 2048
GRID_W = 64
CTX_LEN = 256
FOURIER_WIDTH = 512
FOURIER_GROUPS = 4
FOURIER_GROUP_DIM = FOURIER_WIDTH // FOURIER_GROUPS
RWKV_WIDTH = 512
HEAD_DIM = 64
HEADS = RWKV_WIDTH // HEAD_DIM
DECAY_LORA = 64
AAA_LORA = 64
GATE_LORA = 128
RWKV_COLS = 3 * RWKV_WIDTH + 2 * DECAY_LORA + 2 * AAA_LORA + GATE_LORA
FOURIER_START = RWKV_COLS
GATE_START = RWKV_COLS + FOURIER_WIDTH
D_FF = 2816
NORM_EPS = 1e-6
GN_EPS = 64e-5

COL_R, COL_K, COL_V = 0, RWKV_WIDTH, 2 * RWKV_WIDTH
COL_WD = 3 * RWKV_WIDTH
COL_AD = COL_WD + 2 * DECAY_LORA
COL_GD = COL_AD + 2 * AAA_LORA

LANES = 128
SUBLANES = 8
TOK_TILE = 256
MM_TILE = 512
CHUNK = 64
SEQ_ALL = SEQ + CTX_LEN
N_TILES = SEQ_ALL // TOK_TILE
N_LAT_TILES = SEQ // TOK_TILE
FFN_TILE = 512
FFN_SPLIT = 2
MOD_ROWS = 24
VMEM_LIMIT = 56 * 1024 * 1024

F32 = jnp.float32
BF16 = jnp.bfloat16


def _dot(a, b):
    return jnp.dot(a, b, preferred_element_type=F32)


def _dot_nt(a, b):
    return lax.dot_general(a, b, (((1,), (1,)), ((), ())), preferred_element_type=F32)


def _dot_tn(a, b):
    return lax.dot_general(a, b, (((0,), (0,)), ((), ())), preferred_element_type=F32)


def _dot_hilo(x, e):
    hi = x.astype(BF16)
    lo = (x - hi.astype(F32)).astype(BF16)
    return _dot(hi, e) + _dot(lo, e)


def _dot_split3(e, x):
    x1 = x.astype(BF16)
    rem = x - x1.astype(F32)
    x2 = rem.astype(BF16)
    x3 = (rem - x2.astype(F32)).astype(BF16)
    return _dot(e, x1) + _dot(e, x2) + _dot(e, x3)


def _rms(u):
    return u * lax.rsqrt(jnp.mean(u * u, axis=-1, keepdims=True) + NORM_EPS)


def _sigmoid(z):
    return 1.0 / (1.0 + jnp.exp(-z))


def _softplus(z):
    return jnp.maximum(z, 0.0) + jnp.log(1.0 + jnp.exp(-jnp.abs(z)))


def _const_spec(shape):
    nd = len(shape)
    return pl.BlockSpec(shape, lambda *_: (0,) * nd, pipeline_mode=pl.Buffered(1))


def _mod_kernel(c_ref, w_ref, b_ref, o_ref):
    c = c_ref[...]
    s = (c * _sigmoid(c)).astype(BF16)
    o_ref[...] = _dot(s, w_ref[...]) + b_ref[...]


def _modulation(cc, w_ada, b_ada):
    n_blk = 4
    bn = (6 * D_MODEL) // n_blk
    return pl.pallas_call(
        _mod_kernel,
        out_shape=jax.ShapeDtypeStruct((MOD_ROWS, 6 * D_MODEL), F32),
        grid=(n_blk,),
        in_specs=[
            pl.BlockSpec((MOD_ROWS, D_MODEL), lambda n: (0, 0)),
            pl.BlockSpec((D_MODEL, bn), lambda n: (0, n)),
            pl.BlockSpec((1, bn), lambda n: (0, n)),
        ],
        out_specs=pl.BlockSpec((MOD_ROWS, bn), lambda n: (0, n)),
        name="adaln_mod",
    )(cc, w_ada, b_ada)


def _norm_mod(u, g1_ref, mod_ref):
    shift = mod_ref[0, :, 0:D_MODEL]
    scale = mod_ref[0, :, D_MODEL:2 * D_MODEL]
    return (_rms(u) * g1_ref[...]) * (1.0 + scale) + shift


def _shift_store(pe_ref, rows, mup_ref, mun_ref, p_ref):
    pm = pe_ref[pl.ds(SUBLANES, rows), :]
    pu = pe_ref[pl.ds(SUBLANES - 1, rows), :]
    pd = pe_ref[pl.ds(SUBLANES + 1, rows), :]
    p_ref[0] = pm + mup_ref[...] * (pu - pm) + mun_ref[...] * (pd - pm)


def _inproj_lat_kernel(x_ref, xp_ref, xn_ref, mod_ref, g1_ref, w_ref, mup_ref, mun_ref, p_ref, xf_ref, pe_ref):
    j = pl.program_id(1)
    prev_ok = jnp.where(j >= 1, 1.0, 0.0)
    next_ok = jnp.where(j <= SEQ // MM_TILE - 2, 1.0, 0.0)
    h_ext = jnp.concatenate(
        [_norm_mod(xp_ref[0], g1_ref, mod_ref) * prev_ok, _norm_mod(x_ref[0], g1_ref, mod_ref),
         _norm_mod(xn_ref[0], g1_ref, mod_ref) * next_ok], axis=0).astype(BF16)
    p_ext = _dot(h_ext, w_ref[...])
    xf_ref[0] = p_ext[SUBLANES:SUBLANES + MM_TILE, FOURIER_START:GATE_START].astype(BF16)
    pe_ref[...] = p_ext[:, 0:RWKV_COLS]
    _shift_store(pe_ref, MM_TILE, mup_ref, mun_ref, p_ref)


def _inproj_ctx_kernel(c_ref, mod_ref, g1_ref, w_ref, mup_ref, mun_ref, slab_ref, p_ref, pe_ref):
    del slab_ref
    halo = jnp.zeros((SUBLANES, RWKV_COLS), F32)
    pe_ref[0:SUBLANES, :] = halo
    pe_ref[SUBLANES + CTX_LEN:, :] = halo
    pe_ref[SUBLANES:SUBLANES + CTX_LEN, :] = _dot(_norm_mod(c_ref[0], g1_ref, mod_ref).astype(BF16), w_ref[...])
    _shift_store(pe_ref, CTX_LEN, mup_ref, mun_ref, p_ref)


def _inproj(x, ctx, mod3, g1, w_rf, mu_prev, mu_next):
    halo_blocks = MM_TILE // SUBLANES
    last_halo = SEQ // SUBLANES - 1
    params = pltpu.CompilerParams(vmem_limit_bytes=VMEM_LIMIT)
    slab, xf = pl.pallas_call(
        _inproj_lat_kernel,
        out_shape=(jax.ShapeDtypeStruct((BATCH, SEQ_ALL, RWKV_COLS), F32),
                   jax.ShapeDtypeStruct((BATCH, SEQ, FOURIER_WIDTH), BF16)),
        grid=(BATCH, SEQ // MM_TILE),
        in_specs=[
            pl.BlockSpec((1, MM_TILE, D_MODEL), lambda b, j: (b, j, 0)),
            pl.BlockSpec((1, SUBLANES, D_MODEL), lambda b, j: (b, jnp.maximum(j * halo_blocks - 1, 0), 0)),
            pl.BlockSpec((1, SUBLANES, D_MODEL), lambda b, j: (b, jnp.minimum((j + 1) * halo_blocks, last_halo), 0)),
            pl.BlockSpec((1, 1, 2 * D_MODEL), lambda b, j: (b, 0, 0)),
            _const_spec((1, D_MODEL)),
            _const_spec((D_MODEL, GATE_START)),
            _const_spec((1, RWKV_COLS)),
            _const_spec((1, RWKV_COLS)),
        ],
        out_specs=(
            pl.BlockSpec((1, MM_TILE, RWKV_COLS), lambda b, j: (b, j, 0)),
            pl.BlockSpec((1, MM_TILE, FOURIER_WIDTH), lambda b, j: (b, j, 0)),
        ),
        scratch_shapes=[pltpu.VMEM((MM_TILE + 2 * SUBLANES, RWKV_COLS), F32)],
        compiler_params=params,
        name="inproj_latent",
    )(x, x, x, mod3, g1, w_rf, mu_prev, mu_next)
    slab = pl.pallas_call(
        _inproj_ctx_kernel,
        out_shape=jax.ShapeDtypeStruct((BATCH, SEQ_ALL, RWKV_COLS), F32),
        grid=(BATCH,),
        in_specs=[
            pl.BlockSpec((1, CTX_LEN, D_MODEL), lambda b: (b, 0, 0)),
            pl.BlockSpec((1, 1, 2 * D_MODEL), lambda b: (BATCH, 0, 0)),
            _const_spec((1, D_MODEL)),
            _const_spec((D_MODEL, RWKV_COLS)),
            _const_spec((1, RWKV_COLS)),
            _const_spec((1, RWKV_COLS)),
            pl.BlockSpec(memory_space=pl.ANY),
        ],
        out_specs=pl.BlockSpec((1, CTX_LEN, RWKV_COLS), lambda b: (b, SEQ // CTX_LEN, 0)),
        scratch_shapes=[pltpu.VMEM((CTX_LEN + 2 * SUBLANES, RWKV_COLS), F32)],
        input_output_aliases={6: 0},
        compiler_params=params,
        name="inproj_context",
    )(ctx, mod3, g1, w_rf[:, 0:RWKV_COLS], mu_prev, mu_next, slab)
    return slab, xf


_S_R, _S_V, _S_KD, _S_AL, _S_BE, _S_LD, _S_LGI = range(7)
_N_PLANES = 7
_V_W0F, _V_W0B, _V_A0F, _V_A0B, _V_KK, _V_KA, _V_RK = range(7)


def _wkv_kernel(pf_ref, pb_ref, wlo_ref, alo_ref, vec_ref, eh_ref, esel_ref, tri_ref,
                yf_ref, yb_ref, bsf_ref, bsb_ref, zt_ref, s_ref):
    j = pl.program_id(1)

    @pl.when(j == 0)
    def _():
        zt_ref[...] = jnp.zeros_like(zt_ref)

    k_k = vec_ref[_V_KK:_V_KK + 1, :]
    k_a = vec_ref[_V_KA:_V_KA + 1, :]
    r_k = vec_ref[_V_RK:_V_RK + 1, :]

    for d, (p_ref, bs_ref) in enumerate(((pf_ref, bsf_ref), (pb_ref, bsb_ref))):
        w0 = vec_ref[_V_W0F + d:_V_W0F + d + 1, :]
        a0 = vec_ref[_V_A0F + d:_V_A0F + d + 1, :]
        r = p_ref[0, :, COL_R:COL_R + RWKV_WIDTH]
        k = p_ref[0, :, COL_K:COL_K + RWKV_WIDTH]
        v = p_ref[0, :, COL_V:COL_V + RWKV_WIDTH]
        wa = p_ref[0, :, COL_WD:COL_WD + LANES]
        aa = p_ref[0, :, COL_AD:COL_AD + LANES]
        kq = k * k_k
        ss = _dot((kq * kq).astype(BF16), eh_ref[...])
        kk = kq * lax.rsqrt(jnp.maximum(ss, 1e-24))
        wl = w0 + _dot(jnp.tanh(wa).astype(BF16), wlo_ref[d])
        ld = -jnp.exp(-_softplus(-wl) - 0.5)
        a = _sigmoid(a0 + _dot(aa.astype(BF16), alo_ref[d]))
        kd = k * (1.0 + (a - 1.0) * k_a)
        base = d * _N_PLANES
        s_ref[base + _S_R] = r
        s_ref[base + _S_V] = v
        s_ref[base + _S_KD] = kd
        s_ref[base + _S_AL] = -kk
        s_ref[base + _S_BE] = kk * a
        s_ref[base + _S_LD] = ld
        s_ref[base + _S_LGI] = _dot_split3(tri_ref[d], ld)
        bs_ref[0] = _dot((r * kd * r_k).astype(BF16), esel_ref[...])

    ri = lax.broadcasted_iota(jnp.int32, (CHUNK, LANES), 0)
    li = lax.broadcasted_iota(jnp.int32, (CHUNK, LANES), 1)
    ci = jnp.bitwise_and(li, HEAD_DIM - 1)
    lane_lo = li < HEAD_DIM
    eye = ri == ci
    strict = (ci < ri, ci > ri)
    incl = (ci <= ri, ci >= ri)
    merge_masks = [jnp.right_shift(ri, 1) == jnp.right_shift(ci, 1)]
    for sh in range(1, CHUNK.bit_length() - 1):
        merge_masks.append(jnp.logical_and(jnp.right_shift(ri, sh + 1) == jnp.right_shift(ci, sh + 1),
                                           jnp.right_shift(ri, sh) != jnp.right_shift(ci, sh)))
    n2 = 2 * CHUNK
    same_head = (lax.broadcasted_iota(jnp.int32, (n2, LANES), 0) >= CHUNK) == \
        (lax.broadcasted_iota(jnp.int32, (n2, LANES), 1) >= HEAD_DIM)

    def stack(u):
        return jnp.concatenate([jnp.where(lane_lo, u, 0.0), jnp.where(lane_lo, 0.0, u)], axis=0)

    def load(d, p, r0):
        base = d * _N_PLANES
        ls = slice(p * LANES, (p + 1) * LANES)
        rows = slice(r0, r0 + CHUNK)
        return tuple(s_ref[base + pln, rows, ls] for pln in (_S_LD, _S_LGI, _S_R, _S_V, _S_KD, _S_AL, _S_BE))

    def phase_a(dirs, ins):
        us = range(len(dirs))
        ld, lgi, r, v, kd, al, be = ([ins[u][f] for u in us] for f in range(7))
        m = [lgi[u][CHUNK // 2:CHUNK // 2 + 1, :] for u in us]
        last = [CHUNK - 1 if dirs[u] == 0 else 0 for u in us]
        lgc = [lgi[u][last[u]:last[u] + 1, :] for u in us]
        lge = [lgi[u] - ld[u] for u in us]
        e_m = [jnp.exp(m[u] - lgi[u]) for u in us]
        lhs = [jnp.concatenate([al[u] * jnp.exp(lge[u] - m[u]), r[u] * jnp.exp(lgi[u] - m[u])], axis=0).astype(BF16)
               for u in us]
        rhs = [jnp.concatenate([stack(be[u] * e_m[u]), stack(kd[u] * e_m[u])], axis=0).astype(BF16) for u in us]
        sc = [_dot_nt(lhs[u], rhs[u]) for u in us]
        l_ab = [jnp.where(strict[dirs[u]], sc[u][0:CHUNK, 0:LANES], 0.0) for u in us]

        t_m = [jnp.where(eye, 1.0, jnp.where(merge_masks[0], l_ab[u], 0.0)) for u in us]
        for mk in merge_masks[1:]:
            t_bd = [stack(t_m[u]).astype(BF16) for u in us]
            e_l = [_dot(jnp.where(mk, l_ab[u], 0.0).astype(BF16), t_bd[u]) for u in us]
            t_m = [t_m[u] + _dot(t_m[u].astype(BF16), stack(e_l[u]).astype(BF16)) for u in us]
        t_b = [t_m[u].astype(BF16) for u in us]

        v2 = [stack(v[u]).astype(BF16) for u in us]
        nv = [_dot(jnp.concatenate([jnp.where(strict[dirs[u]], sc[u][0:CHUNK, LANES:2 * LANES], 0.0),
                                    jnp.where(incl[dirs[u]], sc[u][CHUNK:n2, LANES:2 * LANES], 0.0)],
                                   axis=0).astype(BF16), v2[u]) for u in us]
        wu = [_dot(t_b[u], jnp.concatenate([stack(al[u] * jnp.exp(lge[u])), stack(nv[u][0:CHUNK])],
                                           axis=1).astype(BF16)) for u in us]
        m_rb = [jnp.where(incl[dirs[u]], sc[u][CHUNK:n2, 0:LANES], 0.0).astype(BF16) for u in us]
        qy = [_dot(m_rb[u], jnp.concatenate([stack(wu[u][:, 0:LANES]), stack(wu[u][:, LANES:2 * LANES])],
                                            axis=1).astype(BF16)) for u in us]
        q = [(r[u] * jnp.exp(lgi[u]) + qy[u][:, 0:LANES]).astype(BF16) for u in us]
        y0 = [nv[u][CHUNK:n2] + qy[u][:, LANES:2 * LANES] for u in us]
        e_c = [jnp.exp(lgc[u] - lgi[u]) for u in us]
        b_e = [(be[u] * e_c[u]).astype(BF16) for u in us]
        g_l = [jnp.where(same_head, _dot_tn(wu[u][:, 0:LANES].astype(BF16), b_e[u]), 0.0).astype(BF16)
               for u in us]
        h_f = [_dot_tn(jnp.concatenate([wu[u][:, LANES:2 * LANES], v[u]], axis=0).astype(BF16),
                       jnp.concatenate([b_e[u], (kd[u] * e_c[u]).astype(BF16)], axis=0)) for u in us]
        h_t = [jnp.where(lane_lo, h_f[u][0:CHUNK], h_f[u][CHUNK:n2]) for u in us]
        g_c = [jnp.exp(lgc[u]) for u in us]
        return q, y0, g_l, h_t, g_c

    n_chunks = TOK_TILE // CHUNK
    pairs = HEADS // 2
    units = [(d, p, i if d == 0 else n_chunks - 1 - i)
             for i in range(n_chunks) for d in range(2) for p in range(pairs)]
    ins = [load(d, p, c * CHUNK) for d, p, c in units]
    q, y0, g_l, h_t, g_c = phase_a([d for d, _, _ in units], ins)
    y_refs = (yf_ref, yb_ref)
    z = {(d, p): zt_ref[d, p] for d in range(2) for p in range(pairs)}
    for u, (d, p, c) in enumerate(units):
        y_refs[d][0, c * CHUNK:(c + 1) * CHUNK, p * LANES:(p + 1) * LANES] = \
            _dot_nt(q[u], stack(z[d, p]).astype(BF16)) + y0[u]
        z[d, p] = z[d, p] * g_c[u] + _dot(z[d, p].astype(BF16), g_l[u]) + h_t[u]
    for (d, p), val in z.items():
        zt_ref[d, p] = val


def _wkv(p_all, wlo, alo, vecs, eh, esel, tri):
    def fwd_tile(j):
        return jnp.where(j == 0, N_LAT_TILES, j - 1)

    def bwd_tile(j):
        return jnp.where(j == 0, N_LAT_TILES, N_LAT_TILES - j)

    def bwd_out(j):
        return jnp.where(j == 0, N_LAT_TILES - 1, N_LAT_TILES - j)

    y_shape = jax.ShapeDtypeStruct((BATCH, SEQ, RWKV_WIDTH), F32)
    bs_shape = jax.ShapeDtypeStruct((BATCH, SEQ, LANES), F32)
    return pl.pallas_call(
        _wkv_kernel,
        out_shape=(y_shape, y_shape, bs_shape, bs_shape),
        grid=(BATCH, N_TILES),
        in_specs=[
            pl.BlockSpec((1, TOK_TILE, RWKV_COLS), lambda b, j: (b, fwd_tile(j), 0)),
            pl.BlockSpec((1, TOK_TILE, RWKV_COLS), lambda b, j: (b, bwd_tile(j), 0)),
            _const_spec((2, LANES, RWKV_WIDTH)),
            _const_spec((2, LANES, RWKV_WIDTH)),
            _const_spec((SUBLANES, RWKV_WIDTH)),
            _const_spec((RWKV_WIDTH, RWKV_WIDTH)),
            _const_spec((RWKV_WIDTH, LANES)),
            _const_spec((2, TOK_TILE, TOK_TILE)),
        ],
        out_specs=(
            pl.BlockSpec((1, TOK_TILE, RWKV_WIDTH), lambda b, j: (b, jnp.maximum(j - 1, 0), 0)),
            pl.BlockSpec((1, TOK_TILE, RWKV_WIDTH), lambda b, j: (b, bwd_out(j), 0)),
            pl.BlockSpec((1, TOK_TILE, LANES), lambda b, j: (b, jnp.maximum(j - 1, 0), 0)),
            pl.BlockSpec((1, TOK_TILE, LANES), lambda b, j: (b, bwd_out(j), 0)),
        ),
        scratch_shapes=[
            pltpu.VMEM((2, HEADS // 2, HEAD_DIM, LANES), F32),
            pltpu.VMEM((2 * _N_PLANES, TOK_TILE, RWKV_WIDTH), F32),
        ],
        compiler_params=pltpu.CompilerParams(
            dimension_semantics=("arbitrary", "arbitrary"), vmem_limit_bytes=VMEM_LIMIT),
        name="wkv7_chunked",
    )(p_all, p_all, wlo, alo, vecs, eh, esel, tri)


FOURIER_M_TILE = 512


def _fourier_kernel(x_ref, cd_ref, ab_ref, o_ref, rhs_ref):
    @pl.when(pl.program_id(1) == 0)
    def _():
        for g in range(FOURIER_GROUPS):
            gs = slice(g * FOURIER_GROUP_DIM, (g + 1) * FOURIER_GROUP_DIM)
            z = _dot(x_ref[0, :, gs], cd_ref[...])
            rhs_ref[0:SEQ, gs] = z[:, 0:FOURIER_GROUP_DIM].astype(BF16)
            rhs_ref[SEQ:2 * SEQ, gs] = z[:, FOURIER_GROUP_DIM:2 * FOURIER_GROUP_DIM].astype(BF16)

    o_ref[0] = _dot(ab_ref[...], rhs_ref[...]).astype(BF16)


def _fourier(xf, cd, ab):
    return pl.pallas_call(
        _fourier_kernel,
        out_shape=jax.ShapeDtypeStruct((BATCH, SEQ, FOURIER_WIDTH), BF16),
        grid=(BATCH, SEQ // FOURIER_M_TILE),
        in_specs=[
            pl.BlockSpec((1, SEQ, FOURIER_WIDTH), lambda b, m: (b, 0, 0)),
            _const_spec((FOURIER_GROUP_DIM, 2 * FOURIER_GROUP_DIM)),
            pl.BlockSpec((FOURIER_M_TILE, 2 * SEQ), lambda b, m: (m, 0)),
        ],
        out_specs=pl.BlockSpec((1, FOURIER_M_TILE, FOURIER_WIDTH), lambda b, m: (b, m, 0)),
        scratch_shapes=[pltpu.VMEM((2 * SEQ, FOURIER_WIDTH), BF16)],
        compiler_params=pltpu.CompilerParams(
            dimension_semantics=("arbitrary", "arbitrary"), vmem_limit_bytes=VMEM_LIMIT),
        name="fourier_dft",
    )(xf, cd, ab)


def _merge_kernel(x_ref, mod_ref, yf_ref, yb_ref, bsf_ref, bsb_ref, v_ref, gd_ref, fo_ref,
                  g1_ref, wg_ref, g2_ref, wur_ref, wuf_ref, wo_ref, ehm_ref, ex_ref, lng_ref, lnb_ref, o_ref):
    x = x_ref[0]
    shift = mod_ref[0, :, 0:D_MODEL]
    scale = mod_ref[0, :, D_MODEL:2 * D_MODEL]
    gate1 = mod_ref[0, :, 2 * D_MODEL:3 * D_MODEL]
    hx = ((_rms(x) * g1_ref[...]) * (1.0 + scale) + shift).astype(BF16)
    gates = _dot(hx, wg_ref[...])

    y = yf_ref[0] + yb_ref[0]
    mean = _dot(y.astype(BF16), ehm_ref[...])
    dy = y - mean
    var = _dot((dy * dy).astype(BF16), ehm_ref[...])
    o = dy * lax.rsqrt(var + GN_EPS) * lng_ref[...] + lnb_ref[...]
    bonus = _dot((bsf_ref[0] + bsb_ref[0]).astype(BF16), ex_ref[...]) * v_ref[0]
    g = _dot(_sigmoid(gd_ref[0]).astype(BF16), g2_ref[...])
    o = ((o + bonus) * g).astype(BF16)
    r_up = _dot(o, wur_ref[...])
    f_up = _dot(fo_ref[0], wuf_ref[...])
    mix = (_sigmoid(gates[:, 0:D_MODEL]) * f_up + _sigmoid(gates[:, D_MODEL:2 * D_MODEL]) * r_up).astype(BF16)
    o_ref[0] = x + gate1 * _dot(mix, wo_ref[...])


def _merge(x, mod3, yf, yb, bsf, bsb, p_all, fo, g1, wg, g2, wur, wuf, wo, ehm, ex, lng, lnb):
    tok = lambda w: pl.BlockSpec((1, MM_TILE, w), lambda b, t: (b, t, 0))
    return pl.pallas_call(
        _merge_kernel,
        out_shape=jax.ShapeDtypeStruct((BATCH, SEQ, D_MODEL), F32),
        grid=(BATCH, SEQ // MM_TILE),
        in_specs=[
            tok(D_MODEL),
            pl.BlockSpec((1, 1, 6 * D_MODEL), lambda b, t: (b, 0, 0)),
            tok(RWKV_WIDTH), tok(RWKV_WIDTH), tok(LANES), tok(LANES),
            pl.BlockSpec((1, MM_TILE, RWKV_WIDTH), lambda b, t: (b, t, COL_V // RWKV_WIDTH)),
            pl.BlockSpec((1, MM_TILE, GATE_LORA), lambda b, t: (b, t, COL_GD // GATE_LORA)),
            tok(FOURIER_WIDTH),
            _const_spec((1, D_MODEL)),
            _const_spec((D_MODEL, 2 * D_MODEL)),
            _const_spec((GATE_LORA, RWKV_WIDTH)),
            _const_spec((RWKV_WIDTH, D_MODEL)),
            _const_spec((FOURIER_WIDTH, D_MODEL)),
            _const_spec((D_MODEL, D_MODEL)),
            _const_spec((RWKV_WIDTH, RWKV_WIDTH)),
            _const_spec((LANES, RWKV_WIDTH)),
            _const_spec((1, RWKV_WIDTH)),
            _const_spec((1, RWKV_WIDTH)),
        ],
        out_specs=tok(D_MODEL),
        compiler_params=pltpu.CompilerParams(
            dimension_semantics=("arbitrary", "arbitrary"), vmem_limit_bytes=VMEM_LIMIT),
        name="branch_merge",
    )(x, mod3, yf, yb, bsf, bsb, p_all, p_all, fo, g1, wg, g2, wur, wuf, wo, ehm, ex, lng, lnb)


def _ffn_kernel(x_ref, mod_ref, g2_ref, wgu_ref, wd_ref, gf_ref, o_ref):
    x = x_ref[0]
    shift = mod_ref[0, :, 3 * D_MODEL:4 * D_MODEL]
    scale = mod_ref[0, :, 4 * D_MODEL:5 * D_MODEL]
    gate2 = mod_ref[0, :, 5 * D_MODEL:6 * D_MODEL]
    hx = ((_rms(x) * g2_ref[...]) * (1.0 + scale) + shift).astype(BF16)
    part = D_FF // FFN_SPLIT
    acc = jnp.zeros((FFN_TILE, D_MODEL), F32)
    for s in range(FFN_SPLIT):
        gt = _dot(hx, wgu_ref[:, s * part:(s + 1) * part])
        up = _dot(hx, wgu_ref[:, D_FF + s * part:D_FF + (s + 1) * part])
        h = (gt * _sigmoid(gt) * up).astype(BF16)
        acc = acc + _dot(h, wd_ref[s * part:(s + 1) * part, :])
    o_ref[0] = _rms(x + gate2 * acc) * gf_ref[...]


def _ffn(x1, mod3, g2, wgu, wd, gf):
    tiles = SEQ // FFN_TILE
    return pl.pallas_call(
        _ffn_kernel,
        out_shape=jax.ShapeDtypeStruct((BATCH, SEQ, D_MODEL), F32),
        grid=(BATCH, tiles),
        in_specs=[
            pl.BlockSpec((1, FFN_TILE, D_MODEL), lambda b, t: (b, t, 0)),
            pl.BlockSpec((1, 1, 6 * D_MODEL), lambda b, t: (b, 0, 0)),
            _const_spec((1, D_MODEL)),
            _const_spec((D_MODEL, 2 * D_FF)),
            _const_spec((D_FF, D_MODEL)),
            _const_spec((1, D_MODEL)),
        ],
        out_specs=pl.BlockSpec((1, FFN_TILE, D_MODEL), lambda b, t: (b, t, 0)),
        compiler_params=pltpu.CompilerParams(
            dimension_semantics=("arbitrary", "arbitrary"), vmem_limit_bytes=VMEM_LIMIT),
        name="swiglu_final",
    )(x1, mod3, g2, wgu, wd, gf)


@functools.lru_cache(maxsize=None)
def _constants():
    rows = SEQ // GRID_W
    t = np.arange(SEQ)
    tr, tc = t // GRID_W, t % GRID_W
    num = (np.outer(tr, tr) * (GRID_W // rows) + np.outer(tc, tc)) % GRID_W
    ang = 2.0 * np.pi * num / GRID_W
    ab = np.concatenate([np.cos(ang), np.sin(ang)], axis=1)
    jj = np.outer(np.arange(FOURIER_GROUP_DIM), np.arange(FOURIER_GROUP_DIM)) % FOURIER_GROUP_DIM
    ang_c = 2.0 * np.pi * jj / FOURIER_GROUP_DIM
    norm = 1.0 / np.sqrt(rows * GRID_W * FOURIER_GROUP_DIM)
    cd = np.concatenate([np.cos(ang_c), -np.sin(ang_c)], axis=1) * norm
    head = np.arange(RWKV_WIDTH) // HEAD_DIM
    eh = (head[:, None] == head[None, :]).astype(np.float32)
    esel = (head[:, None] == np.arange(LANES)[None, :]).astype(np.float32)
    i = np.arange(TOK_TILE)
    same_chunk = (i[:, None] // CHUNK) == (i[None, :] // CHUNK)
    tri = np.stack([same_chunk & (i[None, :] <= i[:, None]), same_chunk & (i[None, :] >= i[:, None])])
    f32 = lambda u: np.asarray(u, np.float32)
    return dict(ab=f32(ab), cd=f32(cd), eh=f32(eh), ehm=f32(eh / HEAD_DIM), esel=f32(esel), ex=f32(esel.T),
                tri=f32(tri))


def kernel(x, c, ctx, c_ctx, norm1_g, norm2_g, w_ada, b_ada, w_in, mu_prev, mu_next, w0_f, w2_f, a0_f, a2_f, w0_b, w2_b, a0_b, a2_b, g2, k_k, k_a, r_k, lnx_g, lnx_b, w_up_r, w_up_f, w_out, w_gu, w_down, final_norm_g):
    cst = {name: jnp.asarray(val).astype(BF16) for name, val in _constants().items()}
    row = lambda u: u.reshape(1, -1)
    cc = jnp.concatenate(
        [c, c_ctx[None, :], jnp.zeros((MOD_ROWS - BATCH - 1, D_MODEL), F32)], axis=0)
    mod = _modulation(cc, w_ada[0].astype(BF16), row(b_ada[0]))
    mod3 = mod.reshape(MOD_ROWS, 1, 6 * D_MODEL)

    w_in0 = w_in[0].astype(BF16)
    p_all, xf = _inproj(x, ctx, mod3, row(norm1_g[0]), w_in0[:, 0:GATE_START], row(mu_prev[0]), row(mu_next[0]))

    zeros_lora = jnp.zeros((DECAY_LORA, RWKV_WIDTH), F32)
    wlo = jnp.stack([jnp.concatenate([w2_f[0], zeros_lora], 0), jnp.concatenate([zeros_lora, w2_b[0]], 0)]).astype(BF16)
    alo = jnp.stack([jnp.concatenate([a2_f[0], zeros_lora], 0), jnp.concatenate([zeros_lora, a2_b[0]], 0)]).astype(BF16)
    vecs = jnp.stack([w0_f[0], w0_b[0], a0_f[0], a0_b[0], k_k[0], k_a[0], r_k[0].reshape(-1),
                      jnp.zeros((RWKV_WIDTH,), F32)])
    yf, yb, bsf, bsb = _wkv(p_all, wlo, alo, vecs, cst["eh"], cst["esel"], cst["tri"])

    fo = _fourier(xf, cst["cd"], cst["ab"])

    x1 = _merge(x, mod3, yf, yb, bsf, bsb, p_all, fo, row(norm1_g[0]), w_in0[:, GATE_START:],
                g2[0].astype(BF16), w_up_r[0].astype(BF16), w_up_f[0].astype(BF16), w_out[0].astype(BF16),
                cst["ehm"], cst["ex"], row(lnx_g[0]), row(lnx_b[0]))
    return _ffn(x1, mod3, row(norm2_g[0]), w_gu[0].astype(BF16), w_down[0].astype(BF16), row(final_norm_g))
```

```python
import functools

import jax
import jax.numpy as jnp
import numpy as np
from jax import lax
from jax.experimental import pallas as pl
from jax.experimental.pallas import tpu as pltpu

D_MODEL = 1024
BATCH = 16
SEQ = 2048
GRID_W = 64
CTX_LEN = 256
FOURIER_WIDTH = 512
FOURIER_GROUPS = 4
FOURIER_GROUP_DIM = FOURIER_WIDTH // FOURIER_GROUPS
RWKV_WIDTH = 512
HEAD_DIM = 64
HEADS = RWKV_WIDTH // HEAD_DIM
DECAY_LORA = 64
AAA_LORA = 64
GATE_LORA = 128
RWKV_COLS = 3 * RWKV_WIDTH + 2 * DECAY_LORA + 2 * AAA_LORA + GATE_LORA
FOURIER_START = RWKV_COLS
GATE_START = RWKV_COLS + FOURIER_WIDTH
D_FF = 2816
NORM_EPS = 1e-6
GN_EPS = 64e-5
DECAY_SCALE = float(np.exp(-0.5))

COL_R, COL_K, COL_V = 0, RWKV_WIDTH, 2 * RWKV_WIDTH
COL_WD = 3 * RWKV_WIDTH
COL_AD = COL_WD + 2 * DECAY_LORA
COL_GD = COL_AD + 2 * AAA_LORA

LANES = 128
SUBLANES = 8
TOK_TILE = 256
MM_TILE = 512
SHIFT_COLS = 640
CHUNK = 64
SEQ_ALL = SEQ + CTX_LEN
N_TILES = SEQ_ALL // TOK_TILE
N_LAT_TILES = SEQ // TOK_TILE
FFN_TILE = 512
FFN_SPLIT = 2
MOD_ROWS = 24
VMEM_LIMIT = 56 * 1024 * 1024

F32 = jnp.float32
BF16 = jnp.bfloat16


def _dot(a, b):
    return jnp.dot(a, b, preferred_element_type=F32)


def _dot_nt(a, b):
    return lax.dot_general(a, b, (((1,), (1,)), ((), ())), preferred_element_type=F32)


def _dot_tn(a, b):
    return lax.dot_general(a, b, (((0,), (0,)), ((), ())), preferred_element_type=F32)


def _dot_hilo(x, e):
    hi = x.astype(BF16)
    lo = (x - hi.astype(F32)).astype(BF16)
    return _dot(hi, e) + _dot(lo, e)


def _dot_split3(e, x):
    x1 = x.astype(BF16)
    rem = x - x1.astype(F32)
    x2 = rem.astype(BF16)
    x3 = (rem - x2.astype(F32)).astype(BF16)
    return _dot(e, x1) + _dot(e, x2) + _dot(e, x3)


def _rms(u):
    return u * lax.rsqrt(jnp.mean(u * u, axis=-1, keepdims=True) + NORM_EPS)


def _sigmoid(z):
    return 1.0 / (1.0 + jnp.exp(-z))


def _softplus(z):
    return jnp.maximum(z, 0.0) + jnp.log(1.0 + jnp.exp(-jnp.abs(z)))


def _const_spec(shape):
    nd = len(shape)
    return pl.BlockSpec(shape, lambda *_: (0,) * nd, pipeline_mode=pl.Buffered(1))


def _mod_kernel(c_ref, w_ref, b_ref, o_ref):
    c = c_ref[...]
    s = (c * _sigmoid(c)).astype(BF16)
    o_ref[...] = _dot(s, w_ref[...]) + b_ref[...]


def _modulation(cc, w_ada, b_ada):
    n_blk = 4
    bn = (6 * D_MODEL) // n_blk
    return pl.pallas_call(
        _mod_kernel,
        out_shape=jax.ShapeDtypeStruct((MOD_ROWS, 6 * D_MODEL), F32),
        grid=(n_blk,),
        in_specs=[
            pl.BlockSpec((MOD_ROWS, D_MODEL), lambda n: (0, 0)),
            pl.BlockSpec((D_MODEL, bn), lambda n: (0, n)),
            pl.BlockSpec((1, bn), lambda n: (0, n)),
        ],
        out_specs=pl.BlockSpec((MOD_ROWS, bn), lambda n: (0, n)),
        name="adaln_mod",
    )(cc, w_ada, b_ada)


def _norm_mod(u, g1_ref, mod_ref):
    shift = mod_ref[0, :, 0:D_MODEL]
    scale = mod_ref[0, :, D_MODEL:2 * D_MODEL]
    return (_rms(u) * g1_ref[...]) * (1.0 + scale) + shift


def _shift_store(pe_ref, rows, mup_ref, mun_ref, p_ref):
    pm = pe_ref[pl.ds(SUBLANES, rows), :]
    pu = pe_ref[pl.ds(SUBLANES - 1, rows), :]
    pd = pe_ref[pl.ds(SUBLANES + 1, rows), :]
    p_ref[0] = pm + mup_ref[...] * (pu - pm) + mun_ref[...] * (pd - pm)


def _inproj_lat_kernel(x_ref, xp_ref, xn_ref, mod_ref, g1_ref, w_ref, mup_ref, mun_ref, p_ref, xf_ref, pe_ref):
    j = pl.program_id(1)
    prev_ok = jnp.where(j >= 1, 1.0, 0.0)
    next_ok = jnp.where(j <= SEQ // MM_TILE - 2, 1.0, 0.0)
    h_ext = jnp.concatenate(
        [_norm_mod(xp_ref[0], g1_ref, mod_ref) * prev_ok, _norm_mod(x_ref[0], g1_ref, mod_ref),
         _norm_mod(xn_ref[0], g1_ref, mod_ref) * next_ok], axis=0).astype(BF16)
    xf_ref[0] = _dot(h_ext[SUBLANES:SUBLANES + MM_TILE], w_ref[:, FOURIER_START:GATE_START]).astype(BF16)
    for c0 in range(0, RWKV_COLS, SHIFT_COLS):
        cs = slice(c0, c0 + SHIFT_COLS)
        pe_ref[:, cs] = _dot(h_ext, w_ref[:, cs])
        pm = pe_ref[pl.ds(SUBLANES, MM_TILE), cs]
        pu = pe_ref[pl.ds(SUBLANES - 1, MM_TILE), cs]
        pd = pe_ref[pl.ds(SUBLANES + 1, MM_TILE), cs]
        p_ref[0, :, cs] = pm + mup_ref[:, cs] * (pu - pm) + mun_ref[:, cs] * (pd - pm)


def _inproj_ctx_kernel(c_ref, mod_ref, g1_ref, w_ref, mup_ref, mun_ref, slab_ref, p_ref, pe_ref):
    del slab_ref
    halo = jnp.zeros((SUBLANES, RWKV_COLS), F32)
    pe_ref[0:SUBLANES, :] = halo
    pe_ref[SUBLANES + CTX_LEN:, :] = halo
    pe_ref[SUBLANES:SUBLANES + CTX_LEN, :] = _dot(_norm_mod(c_ref[0], g1_ref, mod_ref).astype(BF16), w_ref[...])
    _shift_store(pe_ref, CTX_LEN, mup_ref, mun_ref, p_ref)


def _inproj(x, ctx, mod3, g1, w_rf, mu_prev, mu_next):
    halo_blocks = MM_TILE // SUBLANES
    last_halo = SEQ // SUBLANES - 1
    params = pltpu.CompilerParams(vmem_limit_bytes=VMEM_LIMIT)
    slab, xf = pl.pallas_call(
        _inproj_lat_kernel,
        out_shape=(jax.ShapeDtypeStruct((BATCH, SEQ_ALL, RWKV_COLS), F32),
                   jax.ShapeDtypeStruct((BATCH, SEQ, FOURIER_WIDTH), BF16)),
        grid=(BATCH, SEQ // MM_TILE),
        in_specs=[
            pl.BlockSpec((1, MM_TILE, D_MODEL), lambda b, j: (b, j, 0)),
            pl.BlockSpec((1, SUBLANES, D_MODEL), lambda b, j: (b, jnp.maximum(j * halo_blocks - 1, 0), 0)),
            pl.BlockSpec((1, SUBLANES, D_MODEL), lambda b, j: (b, jnp.minimum((j + 1) * halo_blocks, last_halo), 0)),
            pl.BlockSpec((1, 1, 2 * D_MODEL), lambda b, j: (b, 0, 0)),
            _const_spec((1, D_MODEL)),
            _const_spec((D_MODEL, GATE_START)),
            _const_spec((1, RWKV_COLS)),
            _const_spec((1, RWKV_COLS)),
        ],
        out_specs=(
            pl.BlockSpec((1, MM_TILE, RWKV_COLS), lambda b, j: (b, j, 0)),
            pl.BlockSpec((1, MM_TILE, FOURIER_WIDTH), lambda b, j: (b, j, 0)),
        ),
        scratch_shapes=[pltpu.VMEM((MM_TILE + 2 * SUBLANES, RWKV_COLS), F32)],
        compiler_params=params,
        name="inproj_latent",
    )(x, x, x, mod3, g1, w_rf, mu_prev, mu_next)
    slab = pl.pallas_call(
        _inproj_ctx_kernel,
        out_shape=jax.ShapeDtypeStruct((BATCH, SEQ_ALL, RWKV_COLS), F32),
        grid=(BATCH,),
        in_specs=[
            pl.BlockSpec((1, CTX_LEN, D_MODEL), lambda b: (b, 0, 0)),
            pl.BlockSpec((1, 1, 2 * D_MODEL), lambda b: (BATCH, 0, 0)),
            _const_spec((1, D_MODEL)),
            _const_spec((D_MODEL, RWKV_COLS)),
            _const_spec((1, RWKV_COLS)),
            _const_spec((1, RWKV_COLS)),
            pl.BlockSpec(memory_space=pl.ANY),
        ],
        out_specs=pl.BlockSpec((1, CTX_LEN, RWKV_COLS), lambda b: (b, SEQ // CTX_LEN, 0)),
        scratch_shapes=[pltpu.VMEM((CTX_LEN + 2 * SUBLANES, RWKV_COLS), F32)],
        input_output_aliases={6: 0},
        compiler_params=params,
        name="inproj_context",
    )(ctx, mod3, g1, w_rf[:, 0:RWKV_COLS], mu_prev, mu_next, slab)
    return slab, xf


_S_R, _S_V, _S_KD, _S_AL, _S_BE, _S_LD, _S_LGI = range(7)
_N_PLANES = 7
_V_W0F, _V_W0B, _V_A0F, _V_A0B, _V_KK, _V_KA, _V_RK = range(7)


def _wkv_kernel(pf_ref, pb_ref, wlo_ref, alo_ref, vec_ref, eh_ref, esel_ref, tri_ref,
                yf_ref, yb_ref, bsf_ref, bsb_ref, zt_ref, s_ref):
    j = pl.program_id(1)

    @pl.when(j == 0)
    def _():
        zt_ref[...] = jnp.zeros_like(zt_ref)

    k_k = vec_ref[_V_KK:_V_KK + 1, :]
    k_a = vec_ref[_V_KA:_V_KA + 1, :]
    r_k = vec_ref[_V_RK:_V_RK + 1, :]

    for d, (p_ref, bs_ref) in enumerate(((pf_ref, bsf_ref), (pb_ref, bsb_ref))):
        w0 = vec_ref[_V_W0F + d:_V_W0F + d + 1, :]
        a0 = vec_ref[_V_A0F + d:_V_A0F + d + 1, :]
        r = p_ref[0, :, COL_R:COL_R + RWKV_WIDTH]
        k = p_ref[0, :, COL_K:COL_K + RWKV_WIDTH]
        v = p_ref[0, :, COL_V:COL_V + RWKV_WIDTH]
        wa = p_ref[0, :, COL_WD:COL_WD + LANES]
        aa = p_ref[0, :, COL_AD:COL_AD + LANES]
        kq = k * k_k
        ss = _dot((kq * kq).astype(BF16), eh_ref[...])
        kk = kq * lax.rsqrt(jnp.maximum(ss, 1e-24))
        wl = w0 + _dot(jnp.tanh(wa).astype(BF16), wlo_ref[d])
        ld = -DECAY_SCALE * _sigmoid(wl)
        a = _sigmoid(a0 + _dot(aa.astype(BF16), alo_ref[d]))
        kd = k * (1.0 + (a - 1.0) * k_a)
        base = d * _N_PLANES
        s_ref[base + _S_R] = r
        s_ref[base + _S_V] = v
        s_ref[base + _S_KD] = kd
        s_ref[base + _S_AL] = -kk
        s_ref[base + _S_BE] = kk * a
        s_ref[base + _S_LD] = ld
        s_ref[base + _S_LGI] = _dot_split3(tri_ref[d], ld)
        bs_ref[0] = _dot((r * kd * r_k).astype(BF16), esel_ref[...])

    ri = lax.broadcasted_iota(jnp.int32, (CHUNK, LANES), 0)
    li = lax.broadcasted_iota(jnp.int32, (CHUNK, LANES), 1)
    ci = jnp.bitwise_and(li, HEAD_DIM - 1)
    lane_lo = li < HEAD_DIM
    eye = ri == ci
    strict = (ci < ri, ci > ri)
    incl = (ci <= ri, ci >= ri)
    merge_masks = [jnp.right_shift(ri, 1) == jnp.right_shift(ci, 1)]
    for sh in range(1, CHUNK.bit_length() - 1):
        merge_masks.append(jnp.logical_and(jnp.right_shift(ri, sh + 1) == jnp.right_shift(ci, sh + 1),
                                           jnp.right_shift(ri, sh) != jnp.right_shift(ci, sh)))
    n2 = 2 * CHUNK
    same_head = (lax.broadcasted_iota(jnp.int32, (n2, LANES), 0) >= CHUNK) == \
        (lax.broadcasted_iota(jnp.int32, (n2, LANES), 1) >= HEAD_DIM)

    def stack(u):
        return jnp.concatenate([jnp.where(lane_lo, u, 0.0), jnp.where(lane_lo, 0.0, u)], axis=0)

    def load(d, p, r0):
        base = d * _N_PLANES
        ls = slice(p * LANES, (p + 1) * LANES)
        rows = slice(r0, r0 + CHUNK)
        return tuple(s_ref[base + pln, rows, ls] for pln in (_S_LD, _S_LGI, _S_R, _S_V, _S_KD, _S_AL, _S_BE))

    def phase_a(dirs, ins):
        us = range(len(dirs))
        ld, lgi, r, v, kd, al, be = ([ins[u][f] for u in us] for f in range(7))
        m = [lgi[u][CHUNK // 2:CHUNK // 2 + 1, :] for u in us]
        last = [CHUNK - 1 if dirs[u] == 0 else 0 for u in us]
        lgc = [lgi[u][last[u]:last[u] + 1, :] for u in us]
        lge = [lgi[u] - ld[u] for u in us]
        e_m = [jnp.exp(m[u] - lgi[u]) for u in us]
        lhs = [jnp.concatenate([al[u] * jnp.exp(lge[u] - m[u]), r[u] * jnp.exp(lgi[u] - m[u])], axis=0).astype(BF16)
               for u in us]
        rhs = [jnp.concatenate([stack(be[u] * e_m[u]), stack(kd[u] * e_m[u])], axis=0).astype(BF16) for u in us]
        sc = [_dot_nt(lhs[u], rhs[u]) for u in us]
        l_ab = [jnp.where(strict[dirs[u]], sc[u][0:CHUNK, 0:LANES], 0.0) for u in us]

        t_m = [jnp.where(eye, 1.0, jnp.where(merge_masks[0], l_ab[u], 0.0)) for u in us]
        for mk in merge_masks[1:]:
            t_bd = [stack(t_m[u]).astype(BF16) for u in us]
            e_l = [_dot(jnp.where(mk, l_ab[u], 0.0).astype(BF16), t_bd[u]) for u in us]
            t_m = [t_m[u] + _dot(t_m[u].astype(BF16), stack(e_l[u]).astype(BF16)) for u in us]
        t_b = [t_m[u].astype(BF16) for u in us]

        v2 = [stack(v[u]).astype(BF16) for u in us]
        nv = [_dot(jnp.concatenate([jnp.where(strict[dirs[u]], sc[u][0:CHUNK, LANES:2 * LANES], 0.0),
                                    jnp.where(incl[dirs[u]], sc[u][CHUNK:n2, LANES:2 * LANES], 0.0)],
                                   axis=0).astype(BF16), v2[u]) for u in us]
        wu = [_dot(t_b[u], jnp.concatenate([stack(al[u] * jnp.exp(lge[u])), stack(nv[u][0:CHUNK])],
                                           axis=1).astype(BF16)) for u in us]
        m_rb = [jnp.where(incl[dirs[u]], sc[u][CHUNK:n2, 0:LANES], 0.0).astype(BF16) for u in us]
        qy = [_dot(m_rb[u], jnp.concatenate([stack(wu[u][:, 0:LANES]), stack(wu[u][:, LANES:2 * LANES])],
                                            axis=1).astype(BF16)) for u in us]
        q = [(r[u] * jnp.exp(lgi[u]) + qy[u][:, 0:LANES]).astype(BF16) for u in us]
        y0 = [nv[u][CHUNK:n2] + qy[u][:, LANES:2 * LANES] for u in us]
        e_c = [jnp.exp(lgc[u] - lgi[u]) for u in us]
        b_e = [(be[u] * e_c[u]).astype(BF16) for u in us]
        g_l = [jnp.where(same_head, _dot_tn(wu[u][:, 0:LANES].astype(BF16), b_e[u]), 0.0).astype(BF16)
               for u in us]
        h_f = [_dot_tn(jnp.concatenate([wu[u][:, LANES:2 * LANES], v[u]], axis=0).astype(BF16),
                       jnp.concatenate([b_e[u], (kd[u] * e_c[u]).astype(BF16)], axis=0)) for u in us]
        h_t = [jnp.where(lane_lo, h_f[u][0:CHUNK], h_f[u][CHUNK:n2]) for u in us]
        g_c = [jnp.exp(lgc[u]) for u in us]
        return q, y0, g_l, h_t, g_c

    n_chunks = TOK_TILE // CHUNK
    pairs = HEADS // 2
    units = [(d, p, i if d == 0 else n_chunks - 1 - i)
             for i in range(n_chunks) for d in range(2) for p in range(pairs)]
    ins = [load(d, p, c * CHUNK) for d, p, c in units]
    q, y0, g_l, h_t, g_c = phase_a([d for d, _, _ in units], ins)
    y_refs = (yf_ref, yb_ref)
    z = {(d, p): zt_ref[d, p] for d in range(2) for p in range(pairs)}
    for u, (d, p, c) in enumerate(units):
        y_refs[d][0, c * CHUNK:(c + 1) * CHUNK, p * LANES:(p + 1) * LANES] = \
            _dot_nt(q[u], stack(z[d, p]).astype(BF16)) + y0[u]
        z[d, p] = z[d, p] * g_c[u] + _dot(z[d, p].astype(BF16), g_l[u]) + h_t[u]
    for (d, p), val in z.items():
        zt_ref[d, p] = val


def _wkv(p_all, wlo, alo, vecs, eh, esel, tri):
    def fwd_tile(j):
        return jnp.where(j == 0, N_LAT_TILES, j - 1)

    def bwd_tile(j):
        return jnp.where(j == 0, N_LAT_TILES, N_LAT_TILES - j)

    def bwd_out(j):
        return jnp.where(j == 0, N_LAT_TILES - 1, N_LAT_TILES - j)

    y_shape = jax.ShapeDtypeStruct((BATCH, SEQ, RWKV_WIDTH), F32)
    bs_shape = jax.ShapeDtypeStruct((BATCH, SEQ, LANES), F32)
    return pl.pallas_call(
        _wkv_kernel,
        out_shape=(y_shape, y_shape, bs_shape, bs_shape),
        grid=(BATCH, N_TILES),
        in_specs=[
            pl.BlockSpec((1, TOK_TILE, RWKV_COLS), lambda b, j: (b, fwd_tile(j), 0)),
            pl.BlockSpec((1, TOK_TILE, RWKV_COLS), lambda b, j: (b, bwd_tile(j), 0)),
            _const_spec((2, LANES, RWKV_WIDTH)),
            _const_spec((2, LANES, RWKV_WIDTH)),
            _const_spec((SUBLANES, RWKV_WIDTH)),
            _const_spec((RWKV_WIDTH, RWKV_WIDTH)),
            _const_spec((RWKV_WIDTH, LANES)),
            _const_spec((2, TOK_TILE, TOK_TILE)),
        ],
        out_specs=(
            pl.BlockSpec((1, TOK_TILE, RWKV_WIDTH), lambda b, j: (b, jnp.maximum(j - 1, 0), 0)),
            pl.BlockSpec((1, TOK_TILE, RWKV_WIDTH), lambda b, j: (b, bwd_out(j), 0)),
            pl.BlockSpec((1, TOK_TILE, LANES), lambda b, j: (b, jnp.maximum(j - 1, 0), 0)),
            pl.BlockSpec((1, TOK_TILE, LANES), lambda b, j: (b, bwd_out(j), 0)),
        ),
        scratch_shapes=[
            pltpu.VMEM((2, HEADS // 2, HEAD_DIM, LANES), F32),
            pltpu.VMEM((2 * _N_PLANES, TOK_TILE, RWKV_WIDTH), F32),
        ],
        compiler_params=pltpu.CompilerParams(
            dimension_semantics=("arbitrary", "arbitrary"), vmem_limit_bytes=VMEM_LIMIT),
        name="wkv7_chunked",
    )(p_all, p_all, wlo, alo, vecs, eh, esel, tri)


GRID_H = SEQ // GRID_W


ROW_DFT_LANES = 8192


def _fourier_cc_kernel(x_ref, cd_ref, mc_ref, w_ref, z_ref):
    for g in range(FOURIER_GROUPS):
        gs = slice(g * FOURIER_GROUP_DIM, (g + 1) * FOURIER_GROUP_DIM)
        z = _dot(x_ref[0, :, gs], cd_ref[...])
        z_ref[0, :, gs] = z[:, 0:FOURIER_GROUP_DIM].astype(BF16)
        z_ref[1, :, gs] = z[:, FOURIER_GROUP_DIM:2 * FOURIER_GROUP_DIM].astype(BF16)
    for r in range(GRID_H):
        rows = slice(r * GRID_W, (r + 1) * GRID_W)
        w = _dot(mc_ref[...], jnp.concatenate([z_ref[0, rows, :], z_ref[1, rows, :]], axis=0))
        w_ref[0, 0, rows, :] = w[0:GRID_W].astype(BF16)
        w_ref[0, 1, rows, :] = w[GRID_W:2 * GRID_W].astype(BF16)


def _fourier_row_kernel(w_ref, mr_ref, o_ref):
    o_ref[0] = _dot(mr_ref[...], jnp.concatenate([w_ref[0, 0], w_ref[0, 1]], axis=0)).astype(BF16)


def _fourier(xf, cd, mc, mr):
    params = pltpu.CompilerParams(vmem_limit_bytes=VMEM_LIMIT)
    w = pl.pallas_call(
        _fourier_cc_kernel,
        out_shape=jax.ShapeDtypeStruct((BATCH, 2, SEQ, FOURIER_WIDTH), BF16),
        grid=(BATCH,),
        in_specs=[
            pl.BlockSpec((1, SEQ, FOURIER_WIDTH), lambda b: (b, 0, 0)),
            _const_spec((FOURIER_GROUP_DIM, 2 * FOURIER_GROUP_DIM)),
            _const_spec((2 * GRID_W, 2 * GRID_W)),
        ],
        out_specs=pl.BlockSpec((1, 2, SEQ, FOURIER_WIDTH), lambda b: (b, 0, 0, 0)),
        scratch_shapes=[pltpu.VMEM((2, SEQ, FOURIER_WIDTH), BF16)],
        compiler_params=params,
        name="fourier_chan_col",
    )(xf, cd, mc)
    wide = GRID_W * FOURIER_WIDTH
    y = pl.pallas_call(
        _fourier_row_kernel,
        out_shape=jax.ShapeDtypeStruct((BATCH, GRID_H, wide), BF16),
        grid=(BATCH, wide // ROW_DFT_LANES),
        in_specs=[
            pl.BlockSpec((1, 2, GRID_H, ROW_DFT_LANES), lambda b, n: (b, 0, 0, n)),
            _const_spec((GRID_H, 2 * GRID_H)),
        ],
        out_specs=pl.BlockSpec((1, GRID_H, ROW_DFT_LANES), lambda b, n: (b, 0, n)),
        compiler_params=params,
        name="fourier_row",
    )(w.reshape(BATCH, 2, GRID_H, wide), mr)
    return y.reshape(BATCH, SEQ, FOURIER_WIDTH)


def _merge_kernel(x_ref, mod_ref, yf_ref, yb_ref, bsf_ref, bsb_ref, v_ref, gd_ref, fo_ref,
                  g1_ref, wg_ref, g2_ref, wur_ref, wuf_ref, wo_ref, ehm_ref, ex_ref, lng_ref, lnb_ref, o_ref):
    x = x_ref[0]
    shift = mod_ref[0, :, 0:D_MODEL]
    scale = mod_ref[0, :, D_MODEL:2 * D_MODEL]
    gate1 = mod_ref[0, :, 2 * D_MODEL:3 * D_MODEL]
    hx = ((_rms(x) * g1_ref[...]) * (1.0 + scale) + shift).astype(BF16)
    gates = _dot(hx, wg_ref[...])

    y = yf_ref[0] + yb_ref[0]
    mean = _dot(y.astype(BF16), ehm_ref[...])
    dy = y - mean
    var = _dot((dy * dy).astype(BF16), ehm_ref[...])
    o = dy * lax.rsqrt(var + GN_EPS) * lng_ref[...] + lnb_ref[...]
    bonus = _dot((bsf_ref[0] + bsb_ref[0]).astype(BF16), ex_ref[...]) * v_ref[0]
    g = _dot(_sigmoid(gd_ref[0]).astype(BF16), g2_ref[...])
    o = ((o + bonus) * g).astype(BF16)
    r_up = _dot(o, wur_ref[...])
    f_up = _dot(fo_ref[0], wuf_ref[...])
    mix = (_sigmoid(gates[:, 0:D_MODEL]) * f_up + _sigmoid(gates[:, D_MODEL:2 * D_MODEL]) * r_up).astype(BF16)
    o_ref[0] = x + gate1 * _dot(mix, wo_ref[...])


def _merge(x, mod3, yf, yb, bsf, bsb, p_all, fo, g1, wg, g2, wur, wuf, wo, ehm, ex, lng, lnb):
    tok = lambda w: pl.BlockSpec((1, MM_TILE, w), lambda b, t: (b, t, 0))
    return pl.pallas_call(
        _merge_kernel,
        out_shape=jax.ShapeDtypeStruct((BATCH, SEQ, D_MODEL), F32),
        grid=(BATCH, SEQ // MM_TILE),
        in_specs=[
            tok(D_MODEL),
            pl.BlockSpec((1, 1, 6 * D_MODEL), lambda b, t: (b, 0, 0)),
            tok(RWKV_WIDTH), tok(RWKV_WIDTH), tok(LANES), tok(LANES),
            pl.BlockSpec((1, MM_TILE, RWKV_WIDTH), lambda b, t: (b, t, COL_V // RWKV_WIDTH)),
            pl.BlockSpec((1, MM_TILE, GATE_LORA), lambda b, t: (b, t, COL_GD // GATE_LORA)),
            tok(FOURIER_WIDTH),
            _const_spec((1, D_MODEL)),
            _const_spec((D_MODEL, 2 * D_MODEL)),
            _const_spec((GATE_LORA, RWKV_WIDTH)),
            _const_spec((RWKV_WIDTH, D_MODEL)),
            _const_spec((FOURIER_WIDTH, D_MODEL)),
            _const_spec((D_MODEL, D_MODEL)),
            _const_spec((RWKV_WIDTH, RWKV_WIDTH)),
            _const_spec((LANES, RWKV_WIDTH)),
            _const_spec((1, RWKV_WIDTH)),
            _const_spec((1, RWKV_WIDTH)),
        ],
        out_specs=tok(D_MODEL),
        compiler_params=pltpu.CompilerParams(
            dimension_semantics=("arbitrary", "arbitrary"), vmem_limit_bytes=VMEM_LIMIT),
        name="branch_merge",
    )(x, mod3, yf, yb, bsf, bsb, p_all, p_all, fo, g1, wg, g2, wur, wuf, wo, ehm, ex, lng, lnb)


def _ffn_kernel(x_ref, mod_ref, g2_ref, wgu_ref, wd_ref, gf_ref, o_ref):
    x = x_ref[0]
    shift = mod_ref[0, :, 3 * D_MODEL:4 * D_MODEL]
    scale = mod_ref[0, :, 4 * D_MODEL:5 * D_MODEL]
    gate2 = mod_ref[0, :, 5 * D_MODEL:6 * D_MODEL]
    hx = ((_rms(x) * g2_ref[...]) * (1.0 + scale) + shift).astype(BF16)
    part = D_FF // FFN_SPLIT
    acc = jnp.zeros((FFN_TILE, D_MODEL), F32)
    for s in range(FFN_SPLIT):
        gt = _dot(hx, wgu_ref[:, s * part:(s + 1) * part])
        up = _dot(hx, wgu_ref[:, D_FF + s * part:D_FF + (s + 1) * part])
        h = (gt * _sigmoid(gt) * up).astype(BF16)
        acc = acc + _dot(h, wd_ref[s * part:(s + 1) * part, :])
    o_ref[0] = _rms(x + gate2 * acc) * gf_ref[...]


def _ffn(x1, mod3, g2, wgu, wd, gf):
    tiles = SEQ // FFN_TILE
    return pl.pallas_call(
        _ffn_kernel,
        out_shape=jax.ShapeDtypeStruct((BATCH, SEQ, D_MODEL), F32),
        grid=(BATCH, tiles),
        in_specs=[
            pl.BlockSpec((1, FFN_TILE, D_MODEL), lambda b, t: (b, t, 0)),
            pl.BlockSpec((1, 1, 6 * D_MODEL), lambda b, t: (b, 0, 0)),
            _const_spec((1, D_MODEL)),
            _const_spec((D_MODEL, 2 * D_FF)),
            _const_spec((D_FF, D_MODEL)),
            _const_spec((1, D_MODEL)),
        ],
        out_specs=pl.BlockSpec((1, FFN_TILE, D_MODEL), lambda b, t: (b, t, 0)),
        compiler_params=pltpu.CompilerParams(
            dimension_semantics=("arbitrary", "arbitrary"), vmem_limit_bytes=VMEM_LIMIT),
        name="swiglu_final",
    )(x1, mod3, g2, wgu, wd, gf)


@functools.lru_cache(maxsize=None)
def _constants():
    def dft(n):
        ang = 2.0 * np.pi * (np.outer(np.arange(n), np.arange(n)) % n) / n
        return np.cos(ang), -np.sin(ang)

    a_ch, b_ch = dft(FOURIER_GROUP_DIM)
    a_col, b_col = dft(GRID_W)
    a_row, b_row = dft(GRID_H)
    norm = 1.0 / np.sqrt(GRID_H * GRID_W * FOURIER_GROUP_DIM)
    cd = np.concatenate([a_ch, b_ch], axis=1) * norm
    mc = np.block([[a_col, -b_col], [b_col, a_col]])
    mr = np.concatenate([a_row, -b_row], axis=1)
    head = np.arange(RWKV_WIDTH) // HEAD_DIM
    eh = (head[:, None] == head[None, :]).astype(np.float32)
    esel = (head[:, None] == np.arange(LANES)[None, :]).astype(np.float32)
    i = np.arange(TOK_TILE)
    same_chunk = (i[:, None] // CHUNK) == (i[None, :] // CHUNK)
    tri = np.stack([same_chunk & (i[None, :] <= i[:, None]), same_chunk & (i[None, :] >= i[:, None])])
    f32 = lambda u: np.asarray(u, np.float32)
    return dict(cd=f32(cd), mc=f32(mc), mr=f32(mr), eh=f32(eh), ehm=f32(eh / HEAD_DIM), esel=f32(esel), ex=f32(esel.T),
                tri=f32(tri))


def kernel(x, c, ctx, c_ctx, norm1_g, norm2_g, w_ada, b_ada, w_in, mu_prev, mu_next, w0_f, w2_f, a0_f, a2_f, w0_b, w2_b, a0_b, a2_b, g2, k_k, k_a, r_k, lnx_g, lnx_b, w_up_r, w_up_f, w_out, w_gu, w_down, final_norm_g):
    cst = {name: jnp.asarray(val).astype(BF16) for name, val in _constants().items()}
    row = lambda u: u.reshape(1, -1)
    cc = jnp.concatenate(
        [c, c_ctx[None, :], jnp.zeros((MOD_ROWS - BATCH - 1, D_MODEL), F32)], axis=0)
    mod = _modulation(cc, w_ada[0].astype(BF16), row(b_ada[0]))
    mod3 = mod.reshape(MOD_ROWS, 1, 6 * D_MODEL)

    w_in0 = w_in[0].astype(BF16)
    p_all, xf = _inproj(x, ctx, mod3, row(norm1_g[0]), w_in0[:, 0:GATE_START], row(mu_prev[0]), row(mu_next[0]))

    zeros_lora = jnp.zeros((DECAY_LORA, RWKV_WIDTH), F32)
    wlo = jnp.stack([jnp.concatenate([w2_f[0], zeros_lora], 0), jnp.concatenate([zeros_lora, w2_b[0]], 0)]).astype(BF16)
    alo = jnp.stack([jnp.concatenate([a2_f[0], zeros_lora], 0), jnp.concatenate([zeros_lora, a2_b[0]], 0)]).astype(BF16)
    vecs = jnp.stack([w0_f[0], w0_b[0], a0_f[0], a0_b[0], k_k[0], k_a[0], r_k[0].reshape(-1),
                      jnp.zeros((RWKV_WIDTH,), F32)])
    yf, yb, bsf, bsb = _wkv(p_all, wlo, alo, vecs, cst["eh"], cst["esel"], cst["tri"])

    fo = _fourier(xf, cst["cd"], cst["mc"], cst["mr"])

    x1 = _merge(x, mod3, yf, yb, bsf, bsb, p_all, fo, row(norm1_g[0]), w_in0[:, GATE_START:],
                g2[0].astype(BF16), w_up_r[0].astype(BF16), w_up_f[0].astype(BF16), w_out[0].astype(BF16),
                cst["ehm"], cst["ex"], row(lnx_g[0]), row(lnx_b[0]))
    return _ffn(x1, mod3, row(norm2_g[0]), w_gu[0].astype(BF16), w_down[0].astype(BF16), row(final_norm_g))
```

```python
import functools

import jax
import jax.numpy as jnp
import numpy as np
from jax import lax
from jax.experimental import pallas as pl
from jax.experimental.pallas import tpu as pltpu

D_MODEL = 1024
BATCH = 16
SEQ = 2048
GRID_W = 64
CTX_LEN = 256
FOURIER_WIDTH = 512
FOURIER_GROUPS = 4
FOURIER_GROUP_DIM = FOURIER_WIDTH // FOURIER_GROUPS
RWKV_WIDTH = 512
HEAD_DIM = 64
HEADS = RWKV_WIDTH // HEAD_DIM
DECAY_LORA = 64
AAA_LORA = 64
GATE_LORA = 128
RWKV_COLS = 3 * RWKV_WIDTH + 2 * DECAY_LORA + 2 * AAA_LORA + GATE_LORA
FOURIER_START = RWKV_COLS
GATE_START = RWKV_COLS + FOURIER_WIDTH
D_FF = 2816
NORM_EPS = 1e-6
GN_EPS = 64e-5
DECAY_SCALE = float(np.exp(-0.5))

COL_R, COL_K, COL_V = 0, RWKV_WIDTH, 2 * RWKV_WIDTH
COL_WD = 3 * RWKV_WIDTH
COL_AD = COL_WD + 2 * DECAY_LORA
COL_GD = COL_AD + 2 * AAA_LORA

LANES = 128
SUBLANES = 8
TOK_TILE = 256
MM_TILE = 512
SHIFT_COLS = 640
CHUNK = 64
SEQ_ALL = SEQ + CTX_LEN
N_TILES = SEQ_ALL // TOK_TILE
N_LAT_TILES = SEQ // TOK_TILE
FFN_TILE = 512
FFN_SPLIT = 2
MOD_ROWS = 24
VMEM_LIMIT = 56 * 1024 * 1024

F32 = jnp.float32
BF16 = jnp.bfloat16


def _dot(a, b):
    return jnp.dot(a, b, preferred_element_type=F32)


def _dot_nt(a, b):
    return lax.dot_general(a, b, (((1,), (1,)), ((), ())), preferred_element_type=F32)


def _dot_tn(a, b):
    return lax.dot_general(a, b, (((0,), (0,)), ((), ())), preferred_element_type=F32)


def _dot_hilo(x, e):
    hi = x.astype(BF16)
    lo = (x - hi.astype(F32)).astype(BF16)
    return _dot(hi, e) + _dot(lo, e)


def _dot_split3(e, x):
    x1 = x.astype(BF16)
    rem = x - x1.astype(F32)
    x2 = rem.astype(BF16)
    x3 = (rem - x2.astype(F32)).astype(BF16)
    return _dot(e, x1) + _dot(e, x2) + _dot(e, x3)


def _rms(u):
    return u * lax.rsqrt(jnp.mean(u * u, axis=-1, keepdims=True) + NORM_EPS)


def _sigmoid(z):
    return 1.0 / (1.0 + jnp.exp(-z))


def _softplus(z):
    return jnp.maximum(z, 0.0) + jnp.log(1.0 + jnp.exp(-jnp.abs(z)))


def _const_spec(shape):
    nd = len(shape)
    return pl.BlockSpec(shape, lambda *_: (0,) * nd, pipeline_mode=pl.Buffered(1))


def _mod_kernel(c_ref, w_ref, b_ref, o_ref):
    c = c_ref[...]
    s = (c * _sigmoid(c)).astype(BF16)
    o_ref[...] = _dot(s, w_ref[...]) + b_ref[...]


def _modulation(cc, w_ada, b_ada):
    n_blk = 4
    bn = (6 * D_MODEL) // n_blk
    return pl.pallas_call(
        _mod_kernel,
        out_shape=jax.ShapeDtypeStruct((MOD_ROWS, 6 * D_MODEL), F32),
        grid=(n_blk,),
        in_specs=[
            pl.BlockSpec((MOD_ROWS, D_MODEL), lambda n: (0, 0)),
            pl.BlockSpec((D_MODEL, bn), lambda n: (0, n)),
            pl.BlockSpec((1, bn), lambda n: (0, n)),
        ],
        out_specs=pl.BlockSpec((MOD_ROWS, bn), lambda n: (0, n)),
        name="adaln_mod",
    )(cc, w_ada, b_ada)


def _norm_mod(u, g1_ref, mod_ref):
    shift = mod_ref[0, :, 0:D_MODEL]
    scale = mod_ref[0, :, D_MODEL:2 * D_MODEL]
    return (_rms(u) * g1_ref[...]) * (1.0 + scale) + shift


N_LAT_STEPS = SEQ // MM_TILE


def _inproj_kernel(x_ref, xp_ref, xn_ref, c_ref, modb_ref, modc_ref, g1_ref, w_ref, mup_ref, mun_ref,
                   p_ref, xf_ref, pe_ref):
    j = pl.program_id(1)

    def shift_cols(rows, cs):
        pm = pe_ref[pl.ds(SUBLANES, rows), cs]
        pu = pe_ref[pl.ds(SUBLANES - 1, rows), cs]
        pd = pe_ref[pl.ds(SUBLANES + 1, rows), cs]
        p_ref[0, 0:rows, cs] = pm + mup_ref[:, cs] * (pu - pm) + mun_ref[:, cs] * (pd - pm)

    @pl.when(j < N_LAT_STEPS)
    def _():
        prev_ok = jnp.where(j >= 1, 1.0, 0.0)
        next_ok = jnp.where(j <= N_LAT_STEPS - 2, 1.0, 0.0)
        h_ext = jnp.concatenate(
            [_norm_mod(xp_ref[0], g1_ref, modb_ref) * prev_ok, _norm_mod(x_ref[0], g1_ref, modb_ref),
             _norm_mod(xn_ref[0], g1_ref, modb_ref) * next_ok], axis=0).astype(BF16)
        xf_ref[0] = _dot(h_ext[SUBLANES:SUBLANES + MM_TILE], w_ref[:, FOURIER_START:GATE_START]).astype(BF16)
        for c0 in range(0, RWKV_COLS, SHIFT_COLS):
            cs = slice(c0, c0 + SHIFT_COLS)
            pe_ref[:, cs] = _dot(h_ext, w_ref[:, cs])
            shift_cols(MM_TILE, cs)

    @pl.when(j == N_LAT_STEPS)
    def _():
        hc = _norm_mod(c_ref[0], g1_ref, modc_ref).astype(BF16)
        halo = jnp.zeros((SUBLANES, SHIFT_COLS), F32)
        for c0 in range(0, RWKV_COLS, SHIFT_COLS):
            cs = slice(c0, c0 + SHIFT_COLS)
            pe_ref[0:SUBLANES, cs] = halo
            pe_ref[SUBLANES:SUBLANES + CTX_LEN, cs] = _dot(hc, w_ref[:, cs])
            pe_ref[SUBLANES + CTX_LEN:2 * SUBLANES + CTX_LEN, cs] = halo
            shift_cols(CTX_LEN, cs)


def _inproj(x, ctx, mod3, g1, w_rf, mu_prev, mu_next):
    halo_blocks = MM_TILE // SUBLANES
    last_halo = SEQ // SUBLANES - 1
    lat = lambda j: jnp.minimum(j, N_LAT_STEPS - 1)
    return pl.pallas_call(
        _inproj_kernel,
        out_shape=(jax.ShapeDtypeStruct((BATCH, SEQ_ALL, RWKV_COLS), F32),
                   jax.ShapeDtypeStruct((BATCH, SEQ, FOURIER_WIDTH), BF16)),
        grid=(BATCH, N_LAT_STEPS + 1),
        in_specs=[
            pl.BlockSpec((1, MM_TILE, D_MODEL), lambda b, j: (b, lat(j), 0)),
            pl.BlockSpec((1, SUBLANES, D_MODEL), lambda b, j: (b, jnp.maximum(lat(j) * halo_blocks - 1, 0), 0)),
            pl.BlockSpec((1, SUBLANES, D_MODEL),
                         lambda b, j: (b, jnp.minimum((lat(j) + 1) * halo_blocks, last_halo), 0)),
            pl.BlockSpec((1, CTX_LEN, D_MODEL), lambda b, j: (b, 0, 0)),
            pl.BlockSpec((1, 1, 2 * D_MODEL), lambda b, j: (b, 0, 0)),
            pl.BlockSpec((1, 1, 2 * D_MODEL), lambda b, j: (BATCH, 0, 0)),
            _const_spec((1, D_MODEL)),
            _const_spec((D_MODEL, GATE_START)),
            _const_spec((1, RWKV_COLS)),
            _const_spec((1, RWKV_COLS)),
        ],
        out_specs=(
            pl.BlockSpec((1, MM_TILE, RWKV_COLS), lambda b, j: (b, j, 0)),
            pl.BlockSpec((1, MM_TILE, FOURIER_WIDTH), lambda b, j: (b, lat(j), 0)),
        ),
        scratch_shapes=[pltpu.VMEM((MM_TILE + 2 * SUBLANES, RWKV_COLS), F32)],
        compiler_params=pltpu.CompilerParams(
            dimension_semantics=("arbitrary", "arbitrary"), vmem_limit_bytes=VMEM_LIMIT),
        name="inproj_shift",
    )(x, x, x, ctx, mod3, mod3, g1, w_rf, mu_prev, mu_next)


_S_R, _S_V, _S_KD, _S_AL, _S_BE, _S_LD, _S_LGI = range(7)
_N_PLANES = 7
_V_W0F, _V_W0B, _V_A0F, _V_A0B, _V_KK, _V_KA, _V_RK = range(7)


def _wkv_kernel(pf_ref, pb_ref, wlo_ref, alo_ref, vec_ref, eh_ref, esel_ref, tri_ref,
                yf_ref, yb_ref, bsf_ref, bsb_ref, zt_ref, s_ref):
    j = pl.program_id(1)

    @pl.when(j == 0)
    def _():
        zt_ref[...] = jnp.zeros_like(zt_ref)

    k_k = vec_ref[_V_KK:_V_KK + 1, :]
    k_a = vec_ref[_V_KA:_V_KA + 1, :]
    r_k = vec_ref[_V_RK:_V_RK + 1, :]

    for d, (p_ref, bs_ref) in enumerate(((pf_ref, bsf_ref), (pb_ref, bsb_ref))):
        w0 = vec_ref[_V_W0F + d:_V_W0F + d + 1, :]
        a0 = vec_ref[_V_A0F + d:_V_A0F + d + 1, :]
        r = p_ref[0, :, COL_R:COL_R + RWKV_WIDTH]
        k = p_ref[0, :, COL_K:COL_K + RWKV_WIDTH]
        v = p_ref[0, :, COL_V:COL_V + RWKV_WIDTH]
        wa = p_ref[0, :, COL_WD:COL_WD + LANES]
        aa = p_ref[0, :, COL_AD:COL_AD + LANES]
        kq = k * k_k
        ss = _dot((kq * kq).astype(BF16), eh_ref[...])
        kk = kq * lax.rsqrt(jnp.maximum(ss, 1e-24))
        wl = w0 + _dot(jnp.tanh(wa).astype(BF16), wlo_ref[d])
        ld = -DECAY_SCALE * _sigmoid(wl)
        a = _sigmoid(a0 + _dot(aa.astype(BF16), alo_ref[d]))
        kd = k * (1.0 + (a - 1.0) * k_a)
        base = d * _N_PLANES
        s_ref[base + _S_R] = r
        s_ref[base + _S_V] = v
        s_ref[base + _S_KD] = kd
        s_ref[base + _S_AL] = -kk
        s_ref[base + _S_BE] = kk * a
        s_ref[base + _S_LD] = ld
        s_ref[base + _S_LGI] = _dot_split3(tri_ref[d], ld)
        bs_ref[0] = _dot((r * kd * r_k).astype(BF16), esel_ref[...])

    ri = lax.broadcasted_iota(jnp.int32, (CHUNK, LANES), 0)
    li = lax.broadcasted_iota(jnp.int32, (CHUNK, LANES), 1)
    ci = jnp.bitwise_and(li, HEAD_DIM - 1)
    lane_lo = li < HEAD_DIM
    eye = ri == ci
    strict = (ci < ri, ci > ri)
    incl = (ci <= ri, ci >= ri)
    merge_masks = [jnp.right_shift(ri, 1) == jnp.right_shift(ci, 1)]
    for sh in range(1, CHUNK.bit_length() - 1):
        merge_masks.append(jnp.logical_and(jnp.right_shift(ri, sh + 1) == jnp.right_shift(ci, sh + 1),
                                           jnp.right_shift(ri, sh) != jnp.right_shift(ci, sh)))
    n2 = 2 * CHUNK
    same_head = (lax.broadcasted_iota(jnp.int32, (n2, LANES), 0) >= CHUNK) == \
        (lax.broadcasted_iota(jnp.int32, (n2, LANES), 1) >= HEAD_DIM)

    def stack(u):
        return jnp.concatenate([jnp.where(lane_lo, u, 0.0), jnp.where(lane_lo, 0.0, u)], axis=0)

    def load(d, p, r0):
        base = d * _N_PLANES
        ls = slice(p * LANES, (p + 1) * LANES)
        rows = slice(r0, r0 + CHUNK)
        return tuple(s_ref[base + pln, rows, ls] for pln in (_S_LD, _S_LGI, _S_R, _S_V, _S_KD, _S_AL, _S_BE))

    def phase_a(dirs, ins):
        us = range(len(dirs))
        ld, lgi, r, v, kd, al, be = ([ins[u][f] for u in us] for f in range(7))
        m = [lgi[u][CHUNK // 2:CHUNK // 2 + 1, :] for u in us]
        last = [CHUNK - 1 if dirs[u] == 0 else 0 for u in us]
        lgc = [lgi[u][last[u]:last[u] + 1, :] for u in us]
        lge = [lgi[u] - ld[u] for u in us]
        e_m = [jnp.exp(m[u] - lgi[u]) for u in us]
        lhs = [jnp.concatenate([al[u] * jnp.exp(lge[u] - m[u]), r[u] * jnp.exp(lgi[u] - m[u])], axis=0).astype(BF16)
               for u in us]
        rhs = [jnp.concatenate([stack(be[u] * e_m[u]), stack(kd[u] * e_m[u])], axis=0).astype(BF16) for u in us]
        sc = [_dot_nt(lhs[u], rhs[u]) for u in us]
        l_ab = [jnp.where(strict[dirs[u]], sc[u][0:CHUNK, 0:LANES], 0.0) for u in us]

        t_m = [jnp.where(eye, 1.0, jnp.where(merge_masks[0], l_ab[u], 0.0)) for u in us]
        for mk in merge_masks[1:]:
            t_bd = [stack(t_m[u]).astype(BF16) for u in us]
            e_l = [_dot(jnp.where(mk, l_ab[u], 0.0).astype(BF16), t_bd[u]) for u in us]
            t_m = [t_m[u] + _dot(t_m[u].astype(BF16), stack(e_l[u]).astype(BF16)) for u in us]
        t_b = [t_m[u].astype(BF16) for u in us]

        v2 = [stack(v[u]).astype(BF16) for u in us]
        nv = [_dot(jnp.concatenate([jnp.where(strict[dirs[u]], sc[u][0:CHUNK, LANES:2 * LANES], 0.0),
                                    jnp.where(incl[dirs[u]], sc[u][CHUNK:n2, LANES:2 * LANES], 0.0)],
                                   axis=0).astype(BF16), v2[u]) for u in us]
        wu = [_dot(t_b[u], jnp.concatenate([stack(al[u] * jnp.exp(lge[u])), stack(nv[u][0:CHUNK])],
                                           axis=1).astype(BF16)) for u in us]
        m_rb = [jnp.where(incl[dirs[u]], sc[u][CHUNK:n2, 0:LANES], 0.0).astype(BF16) for u in us]
        qy = [_dot(m_rb[u], jnp.concatenate([stack(wu[u][:, 0:LANES]), stack(wu[u][:, LANES:2 * LANES])],
                                            axis=1).astype(BF16)) for u in us]
        q = [(r[u] * jnp.exp(lgi[u]) + qy[u][:, 0:LANES]).astype(BF16) for u in us]
        y0 = [nv[u][CHUNK:n2] + qy[u][:, LANES:2 * LANES] for u in us]
        e_c = [jnp.exp(lgc[u] - lgi[u]) for u in us]
        b_e = [(be[u] * e_c[u]).astype(BF16) for u in us]
        g_l = [jnp.where(same_head, _dot_tn(wu[u][:, 0:LANES].astype(BF16), b_e[u]), 0.0).astype(BF16)
               for u in us]
        h_f = [_dot_tn(jnp.concatenate([wu[u][:, LANES:2 * LANES], v[u]], axis=0).astype(BF16),
                       jnp.concatenate([b_e[u], (kd[u] * e_c[u]).astype(BF16)], axis=0)) for u in us]
        h_t = [jnp.where(lane_lo, h_f[u][0:CHUNK], h_f[u][CHUNK:n2]) for u in us]
        g_c = [jnp.exp(lgc[u]) for u in us]
        return q, y0, g_l, h_t, g_c

    n_chunks = TOK_TILE // CHUNK
    pairs = HEADS // 2
    units = [(d, p, i if d == 0 else n_chunks - 1 - i)
             for i in range(n_chunks) for d in range(2) for p in range(pairs)]
    ins = [load(d, p, c * CHUNK) for d, p, c in units]
    q, y0, g_l, h_t, g_c = phase_a([d for d, _, _ in units], ins)
    y_refs = (yf_ref, yb_ref)
    z = {(d, p): zt_ref[d, p] for d in range(2) for p in range(pairs)}
    for u, (d, p, c) in enumerate(units):
        y_refs[d][0, c * CHUNK:(c + 1) * CHUNK, p * LANES:(p + 1) * LANES] = \
            _dot_nt(q[u], stack(z[d, p]).astype(BF16)) + y0[u]
        z[d, p] = z[d, p] * g_c[u] + _dot(z[d, p].astype(BF16), g_l[u]) + h_t[u]
    for (d, p), val in z.items():
        zt_ref[d, p] = val


def _wkv(p_all, wlo, alo, vecs, eh, esel, tri):
    def fwd_tile(j):
        return jnp.where(j == 0, N_LAT_TILES, j - 1)

    def bwd_tile(j):
        return jnp.where(j == 0, N_LAT_TILES, N_LAT_TILES - j)

    def bwd_out(j):
        return jnp.where(j == 0, N_LAT_TILES - 1, N_LAT_TILES - j)

    y_shape = jax.ShapeDtypeStruct((BATCH, SEQ, RWKV_WIDTH), F32)
    bs_shape = jax.ShapeDtypeStruct((BATCH, SEQ, LANES), F32)
    return pl.pallas_call(
        _wkv_kernel,
        out_shape=(y_shape, y_shape, bs_shape, bs_shape),
        grid=(BATCH, N_TILES),
        in_specs=[
            pl.BlockSpec((1, TOK_TILE, RWKV_COLS), lambda b, j: (b, fwd_tile(j), 0)),
            pl.BlockSpec((1, TOK_TILE, RWKV_COLS), lambda b, j: (b, bwd_tile(j), 0)),
            _const_spec((2, LANES, RWKV_WIDTH)),
            _const_spec((2, LANES, RWKV_WIDTH)),
            _const_spec((SUBLANES, RWKV_WIDTH)),
            _const_spec((RWKV_WIDTH, RWKV_WIDTH)),
            _const_spec((RWKV_WIDTH, LANES)),
            _const_spec((2, TOK_TILE, TOK_TILE)),
        ],
        out_specs=(
            pl.BlockSpec((1, TOK_TILE, RWKV_WIDTH), lambda b, j: (b, jnp.maximum(j - 1, 0), 0)),
            pl.BlockSpec((1, TOK_TILE, RWKV_WIDTH), lambda b, j: (b, bwd_out(j), 0)),
            pl.BlockSpec((1, TOK_TILE, LANES), lambda b, j: (b, jnp.maximum(j - 1, 0), 0)),
            pl.BlockSpec((1, TOK_TILE, LANES), lambda b, j: (b, bwd_out(j), 0)),
        ),
        scratch_shapes=[
            pltpu.VMEM((2, HEADS // 2, HEAD_DIM, LANES), F32),
            pltpu.VMEM((2 * _N_PLANES, TOK_TILE, RWKV_WIDTH), F32),
        ],
        compiler_params=pltpu.CompilerParams(
            dimension_semantics=("arbitrary", "arbitrary"), vmem_limit_bytes=VMEM_LIMIT),
        name="wkv7_chunked",
    )(p_all, p_all, wlo, alo, vecs, eh, esel, tri)


GRID_H = SEQ // GRID_W
FOURIER_M_TILE = 512


def _fourier_kernel(x_ref, cd_ref, ab_ref, o_ref, rhs_ref):
    @pl.when(pl.program_id(1) == 0)
    def _():
        for g in range(FOURIER_GROUPS):
            gs = slice(g * FOURIER_GROUP_DIM, (g + 1) * FOURIER_GROUP_DIM)
            z = _dot(x_ref[0, :, gs], cd_ref[...])
            rhs_ref[0:SEQ, gs] = z[:, 0:FOURIER_GROUP_DIM].astype(BF16)
            rhs_ref[SEQ:2 * SEQ, gs] = z[:, FOURIER_GROUP_DIM:2 * FOURIER_GROUP_DIM].astype(BF16)

    o_ref[0] = _dot(ab_ref[...], rhs_ref[...]).astype(BF16)


def _fourier(xf, cd, ab):
    return pl.pallas_call(
        _fourier_kernel,
        out_shape=jax.ShapeDtypeStruct((BATCH, SEQ, FOURIER_WIDTH), BF16),
        grid=(BATCH, SEQ // FOURIER_M_TILE),
        in_specs=[
            pl.BlockSpec((1, SEQ, FOURIER_WIDTH), lambda b, m: (b, 0, 0)),
            _const_spec((FOURIER_GROUP_DIM, 2 * FOURIER_GROUP_DIM)),
            pl.BlockSpec((FOURIER_M_TILE, 2 * SEQ), lambda b, m: (m, 0)),
        ],
        out_specs=pl.BlockSpec((1, FOURIER_M_TILE, FOURIER_WIDTH), lambda b, m: (b, m, 0)),
        scratch_shapes=[pltpu.VMEM((2 * SEQ, FOURIER_WIDTH), BF16)],
        compiler_params=pltpu.CompilerParams(
            dimension_semantics=("arbitrary", "arbitrary"), vmem_limit_bytes=VMEM_LIMIT),
        name="fourier_dft",
    )(xf, cd, ab)


def _merge_kernel(x_ref, mod_ref, yf_ref, yb_ref, bsf_ref, bsb_ref, v_ref, gd_ref, fo_ref,
                  g1_ref, wg_ref, g2_ref, wur_ref, wuf_ref, wo_ref, ehm_ref, ex_ref, lng_ref, lnb_ref, o_ref):
    x = x_ref[0]
    shift = mod_ref[0, :, 0:D_MODEL]
    scale = mod_ref[0, :, D_MODEL:2 * D_MODEL]
    gate1 = mod_ref[0, :, 2 * D_MODEL:3 * D_MODEL]
    hx = ((_rms(x) * g1_ref[...]) * (1.0 + scale) + shift).astype(BF16)
    gates = _dot(hx, wg_ref[...])

    y = yf_ref[0] + yb_ref[0]
    mean = _dot(y.astype(BF16), ehm_ref[...])
    dy = y - mean
    var = _dot((dy * dy).astype(BF16), ehm_ref[...])
    o = dy * lax.rsqrt(var + GN_EPS) * lng_ref[...] + lnb_ref[...]
    bonus = _dot((bsf_ref[0] + bsb_ref[0]).astype(BF16), ex_ref[...]) * v_ref[0]
    g = _dot(_sigmoid(gd_ref[0]).astype(BF16), g2_ref[...])
    o = ((o + bonus) * g).astype(BF16)
    r_up = _dot(o, wur_ref[...])
    f_up = _dot(fo_ref[0], wuf_ref[...])
    mix = (_sigmoid(gates[:, 0:D_MODEL]) * f_up + _sigmoid(gates[:, D_MODEL:2 * D_MODEL]) * r_up).astype(BF16)
    o_ref[0] = x + gate1 * _dot(mix, wo_ref[...])


def _merge(x, mod3, yf, yb, bsf, bsb, p_all, fo, g1, wg, g2, wur, wuf, wo, ehm, ex, lng, lnb):
    tok = lambda w: pl.BlockSpec((1, MM_TILE, w), lambda b, t: (b, t, 0))
    return pl.pallas_call(
        _merge_kernel,
        out_shape=jax.ShapeDtypeStruct((BATCH, SEQ, D_MODEL), F32),
        grid=(BATCH, SEQ // MM_TILE),
        in_specs=[
            tok(D_MODEL),
            pl.BlockSpec((1, 1, 6 * D_MODEL), lambda b, t: (b, 0, 0)),
            tok(RWKV_WIDTH), tok(RWKV_WIDTH), tok(LANES), tok(LANES),
            pl.BlockSpec((1, MM_TILE, RWKV_WIDTH), lambda b, t: (b, t, COL_V // RWKV_WIDTH)),
            pl.BlockSpec((1, MM_TILE, GATE_LORA), lambda b, t: (b, t, COL_GD // GATE_LORA)),
            tok(FOURIER_WIDTH),
            _const_spec((1, D_MODEL)),
            _const_spec((D_MODEL, 2 * D_MODEL)),
            _const_spec((GATE_LORA, RWKV_WIDTH)),
            _const_spec((RWKV_WIDTH, D_MODEL)),
            _const_spec((FOURIER_WIDTH, D_MODEL)),
            _const_spec((D_MODEL, D_MODEL)),
            _const_spec((RWKV_WIDTH, RWKV_WIDTH)),
            _const_spec((LANES, RWKV_WIDTH)),
            _const_spec((1, RWKV_WIDTH)),
            _const_spec((1, RWKV_WIDTH)),
        ],
        out_specs=tok(D_MODEL),
        compiler_params=pltpu.CompilerParams(
            dimension_semantics=("arbitrary", "arbitrary"), vmem_limit_bytes=VMEM_LIMIT),
        name="branch_merge",
    )(x, mod3, yf, yb, bsf, bsb, p_all, p_all, fo, g1, wg, g2, wur, wuf, wo, ehm, ex, lng, lnb)


def _ffn_kernel(x_ref, mod_ref, g2_ref, wgu_ref, wd_ref, gf_ref, o_ref):
    x = x_ref[0]
    shift = mod_ref[0, :, 3 * D_MODEL:4 * D_MODEL]
    scale = mod_ref[0, :, 4 * D_MODEL:5 * D_MODEL]
    gate2 = mod_ref[0, :, 5 * D_MODEL:6 * D_MODEL]
    hx = ((_rms(x) * g2_ref[...]) * (1.0 + scale) + shift).astype(BF16)
    part = D_FF // FFN_SPLIT
    acc = jnp.zeros((FFN_TILE, D_MODEL), F32)
    for s in range(FFN_SPLIT):
        gt = _dot(hx, wgu_ref[:, s * part:(s + 1) * part])
        up = _dot(hx, wgu_ref[:, D_FF + s * part:D_FF + (s + 1) * part])
        h = (gt * _sigmoid(gt) * up).astype(BF16)
        acc = acc + _dot(h, wd_ref[s * part:(s + 1) * part, :])
    o_ref[0] = _rms(x + gate2 * acc) * gf_ref[...]


def _ffn(x1, mod3, g2, wgu, wd, gf):
    tiles = SEQ // FFN_TILE
    return pl.pallas_call(
        _ffn_kernel,
        out_shape=jax.ShapeDtypeStruct((BATCH, SEQ, D_MODEL), F32),
        grid=(BATCH, tiles),
        in_specs=[
            pl.BlockSpec((1, FFN_TILE, D_MODEL), lambda b, t: (b, t, 0)),
            pl.BlockSpec((1, 1, 6 * D_MODEL), lambda b, t: (b, 0, 0)),
            _const_spec((1, D_MODEL)),
            _const_spec((D_MODEL, 2 * D_FF)),
            _const_spec((D_FF, D_MODEL)),
            _const_spec((1, D_MODEL)),
        ],
        out_specs=pl.BlockSpec((1, FFN_TILE, D_MODEL), lambda b, t: (b, t, 0)),
        compiler_params=pltpu.CompilerParams(
            dimension_semantics=("arbitrary", "arbitrary"), vmem_limit_bytes=VMEM_LIMIT),
        name="swiglu_final",
    )(x1, mod3, g2, wgu, wd, gf)


@functools.lru_cache(maxsize=None)
def _constants():
    def dft(n):
        ang = 2.0 * np.pi * (np.outer(np.arange(n), np.arange(n)) % n) / n
        return np.cos(ang), -np.sin(ang)

    a_ch, b_ch = dft(FOURIER_GROUP_DIM)
    a_col, b_col = dft(GRID_W)
    a_row, b_row = dft(GRID_H)
    norm = 1.0 / np.sqrt(GRID_H * GRID_W * FOURIER_GROUP_DIM)
    cd = np.concatenate([a_ch, b_ch], axis=1) * norm
    a_tok = np.kron(a_row, a_col) - np.kron(b_row, b_col)
    b_tok = np.kron(a_row, b_col) + np.kron(b_row, a_col)
    ab = np.concatenate([a_tok, -b_tok], axis=1)
    head = np.arange(RWKV_WIDTH) // HEAD_DIM
    eh = (head[:, None] == head[None, :]).astype(np.float32)
    esel = (head[:, None] == np.arange(LANES)[None, :]).astype(np.float32)
    i = np.arange(TOK_TILE)
    same_chunk = (i[:, None] // CHUNK) == (i[None, :] // CHUNK)
    tri = np.stack([same_chunk & (i[None, :] <= i[:, None]), same_chunk & (i[None, :] >= i[:, None])])
    f32 = lambda u: np.asarray(u, np.float32)
    return dict(cd=f32(cd), ab=f32(ab), eh=f32(eh), ehm=f32(eh / HEAD_DIM), esel=f32(esel), ex=f32(esel.T),
                tri=f32(tri))


def kernel(x, c, ctx, c_ctx, norm1_g, norm2_g, w_ada, b_ada, w_in, mu_prev, mu_next, w0_f, w2_f, a0_f, a2_f, w0_b, w2_b, a0_b, a2_b, g2, k_k, k_a, r_k, lnx_g, lnx_b, w_up_r, w_up_f, w_out, w_gu, w_down, final_norm_g):
    cst = {name: jnp.asarray(val).astype(BF16) for name, val in _constants().items()}
    row = lambda u: u.reshape(1, -1)
    cc = jnp.concatenate(
        [c, c_ctx[None, :], jnp.zeros((MOD_ROWS - BATCH - 1, D_MODEL), F32)], axis=0)
    mod = _modulation(cc, w_ada[0].astype(BF16), row(b_ada[0]))
    mod3 = mod.reshape(MOD_ROWS, 1, 6 * D_MODEL)

    w_in0 = w_in[0].astype(BF16)
    p_all, xf = _inproj(x, ctx, mod3, row(norm1_g[0]), w_in0[:, 0:GATE_START], row(mu_prev[0]), row(mu_next[0]))

    zeros_lora = jnp.zeros((DECAY_LORA, RWKV_WIDTH), F32)
    wlo = jnp.stack([jnp.concatenate([w2_f[0], zeros_lora], 0), jnp.concatenate([zeros_lora, w2_b[0]], 0)]).astype(BF16)
    alo = jnp.stack([jnp.concatenate([a2_f[0], zeros_lora], 0), jnp.concatenate([zeros_lora, a2_b[0]], 0)]).astype(BF16)
    vecs = jnp.stack([w0_f[0], w0_b[0], a0_f[0], a0_b[0], k_k[0], k_a[0], r_k[0].reshape(-1),
                      jnp.zeros((RWKV_WIDTH,), F32)])
    yf, yb, bsf, bsb = _wkv(p_all, wlo, alo, vecs, cst["eh"], cst["esel"], cst["tri"])

    fo = _fourier(xf, cst["cd"], cst["ab"])

    x1 = _merge(x, mod3, yf, yb, bsf, bsb, p_all, fo, row(norm1_g[0]), w_in0[:, GATE_START:],
                g2[0].astype(BF16), w_up_r[0].astype(BF16), w_up_f[0].astype(BF16), w_out[0].astype(BF16),
                cst["ehm"], cst["ex"], row(lnx_g[0]), row(lnx_b[0]))
    return _ffn(x1, mod3, row(norm2_g[0]), w_gu[0].astype(BF16), w_down[0].astype(BF16), row(final_norm_g))
```

```python
import functools

import jax
import jax.numpy as jnp
import numpy as np
from jax import lax
from jax.experimental import pallas as pl
from jax.experimental.pallas import tpu as pltpu

D_MODEL = 1024
BATCH = 16
SEQ = 2048
GRID_W = 64
CTX_LEN = 256
FOURIER_WIDTH = 512
FOURIER_GROUPS = 4
FOURIER_GROUP_DIM = FOURIER_WIDTH // FOURIER_GROUPS
RWKV_WIDTH = 512
HEAD_DIM = 64
HEADS = RWKV_WIDTH // HEAD_DIM
DECAY_LORA = 64
AAA_LORA = 64
GATE_LORA = 128
RWKV_COLS = 3 * RWKV_WIDTH + 2 * DECAY_LORA + 2 * AAA_LORA + GATE_LORA
FOURIER_START = RWKV_COLS
GATE_START = RWKV_COLS + FOURIER_WIDTH
D_FF = 2816
NORM_EPS = 1e-6
GN_EPS = 64e-5
DECAY_SCALE = float(np.exp(-0.5))

COL_R, COL_K, COL_V = 0, RWKV_WIDTH, 2 * RWKV_WIDTH
COL_WD = 3 * RWKV_WIDTH
COL_AD = COL_WD + 2 * DECAY_LORA
COL_GD = COL_AD + 2 * AAA_LORA

LANES = 128
SUBLANES = 8
TOK_TILE = 256
MM_TILE = 512
SHIFT_COLS = 1920
CHUNK = 64
SEQ_ALL = SEQ + CTX_LEN
N_TILES = SEQ_ALL // TOK_TILE
N_LAT_TILES = SEQ // TOK_TILE
FFN_TILE = 512
FFN_SPLIT = 11
MOD_ROWS = 24
VMEM_LIMIT = 56 * 1024 * 1024

F32 = jnp.float32
BF16 = jnp.bfloat16


def _dot(a, b):
    return jnp.dot(a, b, preferred_element_type=F32)


def _dot_nt(a, b):
    return lax.dot_general(a, b, (((1,), (1,)), ((), ())), preferred_element_type=F32)


def _dot_tn(a, b):
    return lax.dot_general(a, b, (((0,), (0,)), ((), ())), preferred_element_type=F32)


def _dot_hilo(x, e):
    hi = x.astype(BF16)
    lo = (x - hi.astype(F32)).astype(BF16)
    return _dot(hi, e) + _dot(lo, e)


def _dot_split3(e, x):
    x1 = x.astype(BF16)
    rem = x - x1.astype(F32)
    x2 = rem.astype(BF16)
    x3 = (rem - x2.astype(F32)).astype(BF16)
    return _dot(e, x1) + _dot(e, x2) + _dot(e, x3)


def _rms(u):
    return u * lax.rsqrt(jnp.mean(u * u, axis=-1, keepdims=True) + NORM_EPS)


def _sigmoid(z):
    return 1.0 / (1.0 + jnp.exp(-z))


def _softplus(z):
    return jnp.maximum(z, 0.0) + jnp.log(1.0 + jnp.exp(-jnp.abs(z)))


def _const_spec(shape):
    nd = len(shape)
    return pl.BlockSpec(shape, lambda *_: (0,) * nd, pipeline_mode=pl.Buffered(1))


def _mod_kernel(c_ref, w_ref, b_ref, o_ref):
    c = c_ref[...]
    s = (c * _sigmoid(c)).astype(BF16)
    o_ref[...] = _dot(s, w_ref[...]) + b_ref[...]


def _modulation(cc, w_ada, b_ada):
    n_blk = 4
    bn = (6 * D_MODEL) // n_blk
    return pl.pallas_call(
        _mod_kernel,
        out_shape=jax.ShapeDtypeStruct((MOD_ROWS, 6 * D_MODEL), F32),
        grid=(n_blk,),
        in_specs=[
            pl.BlockSpec((MOD_ROWS, D_MODEL), lambda n: (0, 0)),
            pl.BlockSpec((D_MODEL, bn), lambda n: (0, n)),
            pl.BlockSpec((1, bn), lambda n: (0, n)),
        ],
        out_specs=pl.BlockSpec((MOD_ROWS, bn), lambda n: (0, n)),
        name="adaln_mod",
    )(cc, w_ada, b_ada)


def _norm_mod(u, g1_ref, mod_ref):
    shift = mod_ref[0, :, 0:D_MODEL]
    scale = mod_ref[0, :, D_MODEL:2 * D_MODEL]
    return (_rms(u) * g1_ref[...]) * (1.0 + scale) + shift


N_LAT_STEPS = SEQ // MM_TILE


def _inproj_kernel(x_ref, xp_ref, xn_ref, c_ref, modb_ref, modc_ref, g1_ref, w_ref, mup_ref, mun_ref,
                   p_ref, xf_ref, pe_ref):
    j = pl.program_id(1)

    def shift_cols(rows, cs):
        pm = pe_ref[pl.ds(SUBLANES, rows), cs]
        pu = pe_ref[pl.ds(SUBLANES - 1, rows), cs]
        pd = pe_ref[pl.ds(SUBLANES + 1, rows), cs]
        p_ref[0, 0:rows, cs] = pm + mup_ref[:, cs] * (pu - pm) + mun_ref[:, cs] * (pd - pm)

    @pl.when(j < N_LAT_STEPS)
    def _():
        prev_ok = jnp.where(j >= 1, 1.0, 0.0)
        next_ok = jnp.where(j <= N_LAT_STEPS - 2, 1.0, 0.0)
        h_ext = jnp.concatenate(
            [_norm_mod(xp_ref[0], g1_ref, modb_ref) * prev_ok, _norm_mod(x_ref[0], g1_ref, modb_ref),
             _norm_mod(xn_ref[0], g1_ref, modb_ref) * next_ok], axis=0).astype(BF16)
        xf_ref[0] = _dot(h_ext[SUBLANES:SUBLANES + MM_TILE], w_ref[:, FOURIER_START:GATE_START]).astype(BF16)
        for c0 in range(0, RWKV_COLS, SHIFT_COLS):
            cs = slice(c0, c0 + SHIFT_COLS)
            pe_ref[:, cs] = _dot(h_ext, w_ref[:, cs])
            shift_cols(MM_TILE, cs)

    @pl.when(j == N_LAT_STEPS)
    def _():
        hc = _norm_mod(c_ref[0], g1_ref, modc_ref).astype(BF16)
        halo = jnp.zeros((SUBLANES, SHIFT_COLS), F32)
        for c0 in range(0, RWKV_COLS, SHIFT_COLS):
            cs = slice(c0, c0 + SHIFT_COLS)
            pe_ref[0:SUBLANES, cs] = halo
            pe_ref[SUBLANES:SUBLANES + CTX_LEN, cs] = _dot(hc, w_ref[:, cs])
            pe_ref[SUBLANES + CTX_LEN:2 * SUBLANES + CTX_LEN, cs] = halo
            shift_cols(CTX_LEN, cs)


def _inproj(x, ctx, mod3, g1, w_rf, mu_prev, mu_next):
    halo_blocks = MM_TILE // SUBLANES
    last_halo = SEQ // SUBLANES - 1
    lat = lambda j: jnp.minimum(j, N_LAT_STEPS - 1)
    return pl.pallas_call(
        _inproj_kernel,
        out_shape=(jax.ShapeDtypeStruct((BATCH, SEQ_ALL, RWKV_COLS), F32),
                   jax.ShapeDtypeStruct((BATCH, SEQ, FOURIER_WIDTH), BF16)),
        grid=(BATCH, N_LAT_STEPS + 1),
        in_specs=[
            pl.BlockSpec((1, MM_TILE, D_MODEL), lambda b, j: (b, lat(j), 0)),
            pl.BlockSpec((1, SUBLANES, D_MODEL), lambda b, j: (b, jnp.maximum(lat(j) * halo_blocks - 1, 0), 0)),
            pl.BlockSpec((1, SUBLANES, D_MODEL),
                         lambda b, j: (b, jnp.minimum((lat(j) + 1) * halo_blocks, last_halo), 0)),
            pl.BlockSpec((1, CTX_LEN, D_MODEL), lambda b, j: (b, 0, 0)),
            pl.BlockSpec((1, 1, 2 * D_MODEL), lambda b, j: (b, 0, 0)),
            pl.BlockSpec((1, 1, 2 * D_MODEL), lambda b, j: (BATCH, 0, 0)),
            _const_spec((1, D_MODEL)),
            _const_spec((D_MODEL, GATE_START)),
            _const_spec((1, RWKV_COLS)),
            _const_spec((1, RWKV_COLS)),
        ],
        out_specs=(
            pl.BlockSpec((1, MM_TILE, RWKV_COLS), lambda b, j: (b, j, 0)),
            pl.BlockSpec((1, MM_TILE, FOURIER_WIDTH), lambda b, j: (b, lat(j), 0)),
        ),
        scratch_shapes=[pltpu.VMEM((MM_TILE + 2 * SUBLANES, RWKV_COLS), F32)],
        compiler_params=pltpu.CompilerParams(
            dimension_semantics=("arbitrary", "arbitrary"), vmem_limit_bytes=VMEM_LIMIT),
        name="inproj_shift",
    )(x, x, x, ctx, mod3, mod3, g1, w_rf, mu_prev, mu_next)


_S_R, _S_V, _S_KD, _S_AL, _S_BE, _S_LD, _S_LGI = range(7)
_N_PLANES = 7
_V_W0F, _V_W0B, _V_A0F, _V_A0B, _V_KK, _V_KA, _V_RK = range(7)


def _wkv_kernel(pf_ref, pb_ref, wlo_ref, alo_ref, vec_ref, eh_ref, esel_ref, tri_ref,
                yf_ref, yb_ref, bsf_ref, bsb_ref, zt_ref, s_ref):
    j = pl.program_id(1)

    @pl.when(j == 0)
    def _():
        zt_ref[...] = jnp.zeros_like(zt_ref)

    k_k = vec_ref[_V_KK:_V_KK + 1, :]
    k_a = vec_ref[_V_KA:_V_KA + 1, :]
    r_k = vec_ref[_V_RK:_V_RK + 1, :]

    for d, (p_ref, bs_ref) in enumerate(((pf_ref, bsf_ref), (pb_ref, bsb_ref))):
        w0 = vec_ref[_V_W0F + d:_V_W0F + d + 1, :]
        a0 = vec_ref[_V_A0F + d:_V_A0F + d + 1, :]
        r = p_ref[0, :, COL_R:COL_R + RWKV_WIDTH]
        k = p_ref[0, :, COL_K:COL_K + RWKV_WIDTH]
        v = p_ref[0, :, COL_V:COL_V + RWKV_WIDTH]
        wa = p_ref[0, :, COL_WD:COL_WD + LANES]
        aa = p_ref[0, :, COL_AD:COL_AD + LANES]
        kq = k * k_k
        ss = _dot((kq * kq).astype(BF16), eh_ref[...])
        kk = kq * lax.rsqrt(jnp.maximum(ss, 1e-24))
        wl = w0 + _dot(jnp.tanh(wa).astype(BF16), wlo_ref[d])
        ld = -DECAY_SCALE * _sigmoid(wl)
        a = _sigmoid(a0 + _dot(aa.astype(BF16), alo_ref[d]))
        kd = k * (1.0 + (a - 1.0) * k_a)
        base = d * _N_PLANES
        s_ref[base + _S_R] = r
        s_ref[base + _S_V] = v
        s_ref[base + _S_KD] = kd
        s_ref[base + _S_AL] = -kk
        s_ref[base + _S_BE] = kk * a
        s_ref[base + _S_LD] = ld
        s_ref[base + _S_LGI] = _dot_split3(tri_ref[d], ld)
        bs_ref[0] = _dot((r * kd * r_k).astype(BF16), esel_ref[...])

    ri = lax.broadcasted_iota(jnp.int32, (CHUNK, LANES), 0)
    li = lax.broadcasted_iota(jnp.int32, (CHUNK, LANES), 1)
    ci = jnp.bitwise_and(li, HEAD_DIM - 1)
    lane_lo = li < HEAD_DIM
    eye = ri == ci
    strict = (ci < ri, ci > ri)
    incl = (ci <= ri, ci >= ri)
    merge_masks = [jnp.right_shift(ri, 1) == jnp.right_shift(ci, 1)]
    for sh in range(1, CHUNK.bit_length() - 1):
        merge_masks.append(jnp.logical_and(jnp.right_shift(ri, sh + 1) == jnp.right_shift(ci, sh + 1),
                                           jnp.right_shift(ri, sh) != jnp.right_shift(ci, sh)))
    n2 = 2 * CHUNK
    same_head = (lax.broadcasted_iota(jnp.int32, (n2, LANES), 0) >= CHUNK) == \
        (lax.broadcasted_iota(jnp.int32, (n2, LANES), 1) >= HEAD_DIM)

    def stack(u):
        return jnp.concatenate([jnp.where(lane_lo, u, 0.0), jnp.where(lane_lo, 0.0, u)], axis=0)

    def load(d, p, r0):
        base = d * _N_PLANES
        ls = slice(p * LANES, (p + 1) * LANES)
        rows = slice(r0, r0 + CHUNK)
        return tuple(s_ref[base + pln, rows, ls] for pln in (_S_LD, _S_LGI, _S_R, _S_V, _S_KD, _S_AL, _S_BE))

    def phase_a(dirs, ins):
        us = range(len(dirs))
        ld, lgi, r, v, kd, al, be = ([ins[u][f] for u in us] for f in range(7))
        m = [lgi[u][CHUNK // 2:CHUNK // 2 + 1, :] for u in us]
        last = [CHUNK - 1 if dirs[u] == 0 else 0 for u in us]
        lgc = [lgi[u][last[u]:last[u] + 1, :] for u in us]
        lge = [lgi[u] - ld[u] for u in us]
        e_m = [jnp.exp(m[u] - lgi[u]) for u in us]
        lhs = [jnp.concatenate([al[u] * jnp.exp(lge[u] - m[u]), r[u] * jnp.exp(lgi[u] - m[u])], axis=0).astype(BF16)
               for u in us]
        rhs = [jnp.concatenate([stack(be[u] * e_m[u]), stack(kd[u] * e_m[u])], axis=0).astype(BF16) for u in us]
        sc = [_dot_nt(lhs[u], rhs[u]) for u in us]
        l_ab = [jnp.where(strict[dirs[u]], sc[u][0:CHUNK, 0:LANES], 0.0) for u in us]

        t_m = [jnp.where(eye, 1.0, jnp.where(merge_masks[0], l_ab[u], 0.0)) for u in us]
        for mk in merge_masks[1:]:
            t_bd = [stack(t_m[u]).astype(BF16) for u in us]
            e_l = [_dot(jnp.where(mk, l_ab[u], 0.0).astype(BF16), t_bd[u]) for u in us]
            t_m = [t_m[u] + _dot(t_m[u].astype(BF16), stack(e_l[u]).astype(BF16)) for u in us]
        t_b = [t_m[u].astype(BF16) for u in us]

        v2 = [stack(v[u]).astype(BF16) for u in us]
        nv = [_dot(jnp.concatenate([jnp.where(strict[dirs[u]], sc[u][0:CHUNK, LANES:2 * LANES], 0.0),
                                    jnp.where(incl[dirs[u]], sc[u][CHUNK:n2, LANES:2 * LANES], 0.0)],
                                   axis=0).astype(BF16), v2[u]) for u in us]
        wu = [_dot(t_b[u], jnp.concatenate([stack(al[u] * jnp.exp(lge[u])), stack(nv[u][0:CHUNK])],
                                           axis=1).astype(BF16)) for u in us]
        m_rb = [jnp.where(incl[dirs[u]], sc[u][CHUNK:n2, 0:LANES], 0.0).astype(BF16) for u in us]
        qy = [_dot(m_rb[u], jnp.concatenate([stack(wu[u][:, 0:LANES]), stack(wu[u][:, LANES:2 * LANES])],
                                            axis=1).astype(BF16)) for u in us]
        q = [(r[u] * jnp.exp(lgi[u]) + qy[u][:, 0:LANES]).astype(BF16) for u in us]
        y0 = [nv[u][CHUNK:n2] + qy[u][:, LANES:2 * LANES] for u in us]
        e_c = [jnp.exp(lgc[u] - lgi[u]) for u in us]
        b_e = [(be[u] * e_c[u]).astype(BF16) for u in us]
        g_l = [jnp.where(same_head, _dot_tn(wu[u][:, 0:LANES].astype(BF16), b_e[u]), 0.0).astype(BF16)
               for u in us]
        h_f = [_dot_tn(jnp.concatenate([wu[u][:, LANES:2 * LANES], v[u]], axis=0).astype(BF16),
                       jnp.concatenate([b_e[u], (kd[u] * e_c[u]).astype(BF16)], axis=0)) for u in us]
        h_t = [jnp.where(lane_lo, h_f[u][0:CHUNK], h_f[u][CHUNK:n2]) for u in us]
        g_c = [jnp.exp(lgc[u]) for u in us]
        return q, y0, g_l, h_t, g_c

    n_chunks = TOK_TILE // CHUNK
    pairs = HEADS // 2
    units = [(d, p, i if d == 0 else n_chunks - 1 - i)
             for i in range(n_chunks) for d in range(2) for p in range(pairs)]
    ins = [load(d, p, c * CHUNK) for d, p, c in units]
    q, y0, g_l, h_t, g_c = phase_a([d for d, _, _ in units], ins)
    y_refs = (yf_ref, yb_ref)
    z = {(d, p): zt_ref[d, p] for d in range(2) for p in range(pairs)}
    for u, (d, p, c) in enumerate(units):
        y_refs[d][0, c * CHUNK:(c + 1) * CHUNK, p * LANES:(p + 1) * LANES] = \
            _dot_nt(q[u], stack(z[d, p]).astype(BF16)) + y0[u]
        z[d, p] = z[d, p] * g_c[u] + _dot(z[d, p].astype(BF16), g_l[u]) + h_t[u]
    for (d, p), val in z.items():
        zt_ref[d, p] = val


def _wkv(p_all, wlo, alo, vecs, eh, esel, tri):
    def fwd_tile(j):
        return jnp.where(j == 0, N_LAT_TILES, j - 1)

    def bwd_tile(j):
        return jnp.where(j == 0, N_LAT_TILES, N_LAT_TILES - j)

    def bwd_out(j):
        return jnp.where(j == 0, N_LAT_TILES - 1, N_LAT_TILES - j)

    y_shape = jax.ShapeDtypeStruct((BATCH, SEQ, RWKV_WIDTH), F32)
    bs_shape = jax.ShapeDtypeStruct((BATCH, SEQ, LANES), F32)
    return pl.pallas_call(
        _wkv_kernel,
        out_shape=(y_shape, y_shape, bs_shape, bs_shape),
        grid=(BATCH, N_TILES),
        in_specs=[
            pl.BlockSpec((1, TOK_TILE, RWKV_COLS), lambda b, j: (b, fwd_tile(j), 0)),
            pl.BlockSpec((1, TOK_TILE, RWKV_COLS), lambda b, j: (b, bwd_tile(j), 0)),
            _const_spec((2, LANES, RWKV_WIDTH)),
            _const_spec((2, LANES, RWKV_WIDTH)),
            _const_spec((SUBLANES, RWKV_WIDTH)),
            _const_spec((RWKV_WIDTH, RWKV_WIDTH)),
            _const_spec((RWKV_WIDTH, LANES)),
            _const_spec((2, TOK_TILE, TOK_TILE)),
        ],
        out_specs=(
            pl.BlockSpec((1, TOK_TILE, RWKV_WIDTH), lambda b, j: (b, jnp.maximum(j - 1, 0), 0)),
            pl.BlockSpec((1, TOK_TILE, RWKV_WIDTH), lambda b, j: (b, bwd_out(j), 0)),
            pl.BlockSpec((1, TOK_TILE, LANES), lambda b, j: (b, jnp.maximum(j - 1, 0), 0)),
            pl.BlockSpec((1, TOK_TILE, LANES), lambda b, j: (b, bwd_out(j), 0)),
        ),
        scratch_shapes=[
            pltpu.VMEM((2, HEADS // 2, HEAD_DIM, LANES), F32),
            pltpu.VMEM((2 * _N_PLANES, TOK_TILE, RWKV_WIDTH), F32),
        ],
        compiler_params=pltpu.CompilerParams(
            dimension_semantics=("arbitrary", "arbitrary"), vmem_limit_bytes=VMEM_LIMIT),
        name="wkv7_chunked",
    )(p_all, p_all, wlo, alo, vecs, eh, esel, tri)


GRID_H = SEQ // GRID_W
FOURIER_M_TILE = 512


def _fourier_kernel(x_ref, cd_ref, ab_ref, o_ref, rhs_ref):
    @pl.when(pl.program_id(1) == 0)
    def _():
        for g in range(FOURIER_GROUPS):
            gs = slice(g * FOURIER_GROUP_DIM, (g + 1) * FOURIER_GROUP_DIM)
            z = _dot(x_ref[0, :, gs], cd_ref[...])
            rhs_ref[0:SEQ, gs] = z[:, 0:FOURIER_GROUP_DIM].astype(BF16)
            rhs_ref[SEQ:2 * SEQ, gs] = z[:, FOURIER_GROUP_DIM:2 * FOURIER_GROUP_DIM].astype(BF16)

    o_ref[0] = _dot(ab_ref[...], rhs_ref[...]).astype(BF16)


def _fourier(xf, cd, ab):
    return pl.pallas_call(
        _fourier_kernel,
        out_shape=jax.ShapeDtypeStruct((BATCH, SEQ, FOURIER_WIDTH), BF16),
        grid=(BATCH, SEQ // FOURIER_M_TILE),
        in_specs=[
            pl.BlockSpec((1, SEQ, FOURIER_WIDTH), lambda b, m: (b, 0, 0)),
            _const_spec((FOURIER_GROUP_DIM, 2 * FOURIER_GROUP_DIM)),
            pl.BlockSpec((FOURIER_M_TILE, 2 * SEQ), lambda b, m: (m, 0)),
        ],
        out_specs=pl.BlockSpec((1, FOURIER_M_TILE, FOURIER_WIDTH), lambda b, m: (b, m, 0)),
        scratch_shapes=[pltpu.VMEM((2 * SEQ, FOURIER_WIDTH), BF16)],
        compiler_params=pltpu.CompilerParams(
            dimension_semantics=("arbitrary", "arbitrary"), vmem_limit_bytes=VMEM_LIMIT),
        name="fourier_dft",
    )(xf, cd, ab)


def _merge_kernel(x_ref, mod_ref, yf_ref, yb_ref, bsf_ref, bsb_ref, v_ref, gd_ref, fo_ref,
                  g1_ref, wg_ref, g2_ref, wur_ref, wuf_ref, wo_ref, ehm_ref, ex_ref, lng_ref, lnb_ref, o_ref):
    x = x_ref[0]
    shift = mod_ref[0, :, 0:D_MODEL]
    scale = mod_ref[0, :, D_MODEL:2 * D_MODEL]
    gate1 = mod_ref[0, :, 2 * D_MODEL:3 * D_MODEL]
    hx = ((_rms(x) * g1_ref[...]) * (1.0 + scale) + shift).astype(BF16)
    gates = _dot(hx, wg_ref[...])

    y = yf_ref[0] + yb_ref[0]
    mean = _dot(y.astype(BF16), ehm_ref[...])
    dy = y - mean
    var = _dot((dy * dy).astype(BF16), ehm_ref[...])
    o = dy * lax.rsqrt(var + GN_EPS) * lng_ref[...] + lnb_ref[...]
    bonus = _dot((bsf_ref[0] + bsb_ref[0]).astype(BF16), ex_ref[...]) * v_ref[0]
    g = _dot(_sigmoid(gd_ref[0]).astype(BF16), g2_ref[...])
    o = ((o + bonus) * g).astype(BF16)
    r_up = _dot(o, wur_ref[...])
    f_up = _dot(fo_ref[0], wuf_ref[...])
    mix = (_sigmoid(gates[:, 0:D_MODEL]) * f_up + _sigmoid(gates[:, D_MODEL:2 * D_MODEL]) * r_up).astype(BF16)
    o_ref[0] = x + gate1 * _dot(mix, wo_ref[...])


def _merge(x, mod3, yf, yb, bsf, bsb, p_all, fo, g1, wg, g2, wur, wuf, wo, ehm, ex, lng, lnb):
    tok = lambda w: pl.BlockSpec((1, MM_TILE, w), lambda b, t: (b, t, 0))
    return pl.pallas_call(
        _merge_kernel,
        out_shape=jax.ShapeDtypeStruct((BATCH, SEQ, D_MODEL), F32),
        grid=(BATCH, SEQ // MM_TILE),
        in_specs=[
            tok(D_MODEL),
            pl.BlockSpec((1, 1, 6 * D_MODEL), lambda b, t: (b, 0, 0)),
            tok(RWKV_WIDTH), tok(RWKV_WIDTH), tok(LANES), tok(LANES),
            pl.BlockSpec((1, MM_TILE, RWKV_WIDTH), lambda b, t: (b, t, COL_V // RWKV_WIDTH)),
            pl.BlockSpec((1, MM_TILE, GATE_LORA), lambda b, t: (b, t, COL_GD // GATE_LORA)),
            tok(FOURIER_WIDTH),
            _const_spec((1, D_MODEL)),
            _const_spec((D_MODEL, 2 * D_MODEL)),
            _const_spec((GATE_LORA, RWKV_WIDTH)),
            _const_spec((RWKV_WIDTH, D_MODEL)),
            _const_spec((FOURIER_WIDTH, D_MODEL)),
            _const_spec((D_MODEL, D_MODEL)),
            _const_spec((RWKV_WIDTH, RWKV_WIDTH)),
            _const_spec((LANES, RWKV_WIDTH)),
            _const_spec((1, RWKV_WIDTH)),
            _const_spec((1, RWKV_WIDTH)),
        ],
        out_specs=tok(D_MODEL),
        compiler_params=pltpu.CompilerParams(
            dimension_semantics=("arbitrary", "arbitrary"), vmem_limit_bytes=VMEM_LIMIT),
        name="branch_merge",
    )(x, mod3, yf, yb, bsf, bsb, p_all, p_all, fo, g1, wg, g2, wur, wuf, wo, ehm, ex, lng, lnb)


def _ffn_kernel(x_ref, mod_ref, g2_ref, wgu_ref, wd_ref, gf_ref, o_ref):
    x = x_ref[0]
    shift = mod_ref[0, :, 3 * D_MODEL:4 * D_MODEL]
    scale = mod_ref[0, :, 4 * D_MODEL:5 * D_MODEL]
    gate2 = mod_ref[0, :, 5 * D_MODEL:6 * D_MODEL]
    hx = ((_rms(x) * g2_ref[...]) * (1.0 + scale) + shift).astype(BF16)
    part = D_FF // FFN_SPLIT
    acc = jnp.zeros((FFN_TILE, D_MODEL), F32)
    for s in range(FFN_SPLIT):
        gt = _dot(hx, wgu_ref[:, s * part:(s + 1) * part])
        up = _dot(hx, wgu_ref[:, D_FF + s * part:D_FF + (s + 1) * part])
        h = (gt * _sigmoid(gt) * up).astype(BF16)
        acc = acc + _dot(h, wd_ref[s * part:(s + 1) * part, :])
    o_ref[0] = _rms(x + gate2 * acc) * gf_ref[...]


def _ffn(x1, mod3, g2, wgu, wd, gf):
    tiles = SEQ // FFN_TILE
    return pl.pallas_call(
        _ffn_kernel,
        out_shape=jax.ShapeDtypeStruct((BATCH, SEQ, D_MODEL), F32),
        grid=(BATCH, tiles),
        in_specs=[
            pl.BlockSpec((1, FFN_TILE, D_MODEL), lambda b, t: (b, t, 0)),
            pl.BlockSpec((1, 1, 6 * D_MODEL), lambda b, t: (b, 0, 0)),
            _const_spec((1, D_MODEL)),
            _const_spec((D_MODEL, 2 * D_FF)),
            _const_spec((D_FF, D_MODEL)),
            _const_spec((1, D_MODEL)),
        ],
        out_specs=pl.BlockSpec((1, FFN_TILE, D_MODEL), lambda b, t: (b, t, 0)),
        compiler_params=pltpu.CompilerParams(
            dimension_semantics=("arbitrary", "arbitrary"), vmem_limit_bytes=VMEM_LIMIT),
        name="swiglu_final",
    )(x1, mod3, g2, wgu, wd, gf)


@functools.lru_cache(maxsize=None)
def _constants():
    def dft(n):
        ang = 2.0 * np.pi * (np.outer(np.arange(n), np.arange(n)) % n) / n
        return np.cos(ang), -np.sin(ang)

    a_ch, b_ch = dft(FOURIER_GROUP_DIM)
    a_col, b_col = dft(GRID_W)
    a_row, b_row = dft(GRID_H)
    norm = 1.0 / np.sqrt(GRID_H * GRID_W * FOURIER_GROUP_DIM)
    cd = np.concatenate([a_ch, b_ch], axis=1) * norm
    a_tok = np.kron(a_row, a_col) - np.kron(b_row, b_col)
    b_tok = np.kron(a_row, b_col) + np.kron(b_row, a_col)
    ab = np.concatenate([a_tok, -b_tok], axis=1)
    head = np.arange(RWKV_WIDTH) // HEAD_DIM
    eh = (head[:, None] == head[None, :]).astype(np.float32)
    esel = (head[:, None] == np.arange(LANES)[None, :]).astype(np.float32)
    i = np.arange(TOK_TILE)
    same_chunk = (i[:, None] // CHUNK) == (i[None, :] // CHUNK)
    tri = np.stack([same_chunk & (i[None, :] <= i[:, None]), same_chunk & (i[None, :] >= i[:, None])])
    f32 = lambda u: np.asarray(u, np.float32)
    return dict(cd=f32(cd), ab=f32(ab), eh=f32(eh), ehm=f32(eh / HEAD_DIM), esel=f32(esel), ex=f32(esel.T),
                tri=f32(tri))


def kernel(x, c, ctx, c_ctx, norm1_g, norm2_g, w_ada, b_ada, w_in, mu_prev, mu_next, w0_f, w2_f, a0_f, a2_f, w0_b, w2_b, a0_b, a2_b, g2, k_k, k_a, r_k, lnx_g, lnx_b, w_up_r, w_up_f, w_out, w_gu, w_down, final_norm_g):
    cst = {name: jnp.asarray(val).astype(BF16) for name, val in _constants().items()}
    row = lambda u: u.reshape(1, -1)
    cc = jnp.concatenate(
        [c, c_ctx[None, :], jnp.zeros((MOD_ROWS - BATCH - 1, D_MODEL), F32)], axis=0)
    mod = _modulation(cc, w_ada[0].astype(BF16), row(b_ada[0]))
    mod3 = mod.reshape(MOD_ROWS, 1, 6 * D_MODEL)

    w_in0 = w_in[0].astype(BF16)
    p_all, xf = _inproj(x, ctx, mod3, row(norm1_g[0]), w_in0[:, 0:GATE_START], row(mu_prev[0]), row(mu_next[0]))

    zeros_lora = jnp.zeros((DECAY_LORA, RWKV_WIDTH), F32)
    wlo = jnp.stack([jnp.concatenate([w2_f[0], zeros_lora], 0), jnp.concatenate([zeros_lora, w2_b[0]], 0)]).astype(BF16)
    alo = jnp.stack([jnp.concatenate([a2_f[0], zeros_lora], 0), jnp.concatenate([zeros_lora, a2_b[0]], 0)]).astype(BF16)
    vecs = jnp.stack([w0_f[0], w0_b[0], a0_f[0], a0_b[0], k_k[0], k_a[0], r_k[0].reshape(-1),
                      jnp.zeros((RWKV_WIDTH,), F32)])
    yf, yb, bsf, bsb = _wkv(p_all, wlo, alo, vecs, cst["eh"], cst["esel"], cst["tri"])

    fo = _fourier(xf, cst["cd"], cst["ab"])

    x1 = _merge(x, mod3, yf, yb, bsf, bsb, p_all, fo, row(norm1_g[0]), w_in0[:, GATE_START:],
                g2[0].astype(BF16), w_up_r[0].astype(BF16), w_up_f[0].astype(BF16), w_out[0].astype(BF16),
                cst["ehm"], cst["ex"], row(lnx_g[0]), row(lnx_b[0]))
    return _ffn(x1, mod3, row(norm2_g[0]), w_gu[0].astype(BF16), w_down[0].astype(BF16), row(final_norm_g))
```

```python
import functools

import jax
import jax.numpy as jnp
import numpy as np
from jax import lax
from jax.experimental import pallas as pl
from jax.experimental.pallas import tpu as pltpu

D_MODEL = 1024
BATCH = 16
SEQ = 2048
GRID_W = 64
CTX_LEN = 256
FOURIER_WIDTH = 512
FOURIER_GROUPS = 4
FOURIER_GROUP_DIM = FOURIER_WIDTH // FOURIER_GROUPS
RWKV_WIDTH = 512
HEAD_DIM = 64
HEADS = RWKV_WIDTH // HEAD_DIM
DECAY_LORA = 64
AAA_LORA = 64
GATE_LORA = 128
RWKV_COLS = 3 * RWKV_WIDTH + 2 * DECAY_LORA + 2 * AAA_LORA + GATE_LORA
FOURIER_START = RWKV_COLS
GATE_START = RWKV_COLS + FOURIER_WIDTH
D_FF = 2816
NORM_EPS = 1e-6
GN_EPS = 64e-5
DECAY_SCALE = float(np.exp(-0.5))

COL_R, COL_K, COL_V = 0, RWKV_WIDTH, 2 * RWKV_WIDTH
COL_WD = 3 * RWKV_WIDTH
COL_AD = COL_WD + 2 * DECAY_LORA
COL_GD = COL_AD + 2 * AAA_LORA

LANES = 128
SUBLANES = 8
TOK_TILE = 256
MM_TILE = 512
SHIFT_COLS = 1920
CHUNK = 64
SEQ_ALL = SEQ + CTX_LEN
N_TILES = SEQ_ALL // TOK_TILE
N_LAT_TILES = SEQ // TOK_TILE
FFN_TILE = 512
FFN_SPLIT = 11
MOD_ROWS = 24
VMEM_LIMIT = 56 * 1024 * 1024

F32 = jnp.float32
BF16 = jnp.bfloat16


def _dot(a, b):
    return jnp.dot(a, b, preferred_element_type=F32)


def _dot_nt(a, b):
    return lax.dot_general(a, b, (((1,), (1,)), ((), ())), preferred_element_type=F32)


def _dot_tn(a, b):
    return lax.dot_general(a, b, (((0,), (0,)), ((), ())), preferred_element_type=F32)


def _dot_hilo(x, e):
    hi = x.astype(BF16)
    lo = (x - hi.astype(F32)).astype(BF16)
    return _dot(hi, e) + _dot(lo, e)


def _dot_split3(e, x):
    x1 = x.astype(BF16)
    rem = x - x1.astype(F32)
    x2 = rem.astype(BF16)
    x3 = (rem - x2.astype(F32)).astype(BF16)
    return _dot(e, x1) + _dot(e, x2) + _dot(e, x3)


def _rms(u):
    return u * lax.rsqrt(jnp.mean(u * u, axis=-1, keepdims=True) + NORM_EPS)


def _sigmoid(z):
    return 1.0 / (1.0 + jnp.exp(-z))


def _softplus(z):
    return jnp.maximum(z, 0.0) + jnp.log(1.0 + jnp.exp(-jnp.abs(z)))


def _const_spec(shape):
    nd = len(shape)
    return pl.BlockSpec(shape, lambda *_: (0,) * nd, pipeline_mode=pl.Buffered(1))


def _mod_kernel(c_ref, w_ref, b_ref, o_ref):
    c = c_ref[...]
    s = (c * _sigmoid(c)).astype(BF16)
    o_ref[...] = _dot(s, w_ref[...]) + b_ref[...]


def _modulation(cc, w_ada, b_ada):
    n_blk = 4
    bn = (6 * D_MODEL) // n_blk
    return pl.pallas_call(
        _mod_kernel,
        out_shape=jax.ShapeDtypeStruct((MOD_ROWS, 6 * D_MODEL), F32),
        grid=(n_blk,),
        in_specs=[
            pl.BlockSpec((MOD_ROWS, D_MODEL), lambda n: (0, 0)),
            pl.BlockSpec((D_MODEL, bn), lambda n: (0, n)),
            pl.BlockSpec((1, bn), lambda n: (0, n)),
        ],
        out_specs=pl.BlockSpec((MOD_ROWS, bn), lambda n: (0, n)),
        name="adaln_mod",
    )(cc, w_ada, b_ada)


def _norm_mod(u, g1_ref, mod_ref):
    shift = mod_ref[0, :, 0:D_MODEL]
    scale = mod_ref[0, :, D_MODEL:2 * D_MODEL]
    return (_rms(u) * g1_ref[...]) * (1.0 + scale) + shift


N_LAT_STEPS = SEQ // MM_TILE


def _inproj_kernel(x_ref, xp_ref, xn_ref, c_ref, modb_ref, modc_ref, g1_ref, w_ref, mup_ref, mun_ref,
                   p_ref, xf_ref, pe_ref):
    j = pl.program_id(1)

    def shift_cols(rows, cs):
        pm = pe_ref[pl.ds(SUBLANES, rows), cs]
        pu = pe_ref[pl.ds(SUBLANES - 1, rows), cs]
        pd = pe_ref[pl.ds(SUBLANES + 1, rows), cs]
        p_ref[0, 0:rows, cs] = pm + mup_ref[:, cs] * (pu - pm) + mun_ref[:, cs] * (pd - pm)

    @pl.when(j < N_LAT_STEPS)
    def _():
        prev_ok = jnp.where(j >= 1, 1.0, 0.0)
        next_ok = jnp.where(j <= N_LAT_STEPS - 2, 1.0, 0.0)
        h_ext = jnp.concatenate(
            [_norm_mod(xp_ref[0], g1_ref, modb_ref) * prev_ok, _norm_mod(x_ref[0], g1_ref, modb_ref),
             _norm_mod(xn_ref[0], g1_ref, modb_ref) * next_ok], axis=0).astype(BF16)
        xf_ref[0] = _dot(h_ext[SUBLANES:SUBLANES + MM_TILE], w_ref[:, FOURIER_START:GATE_START]).astype(BF16)
        for c0 in range(0, RWKV_COLS, SHIFT_COLS):
            cs = slice(c0, c0 + SHIFT_COLS)
            pe_ref[:, cs] = _dot(h_ext, w_ref[:, cs])
            shift_cols(MM_TILE, cs)

    @pl.when(j == N_LAT_STEPS)
    def _():
        hc = _norm_mod(c_ref[0], g1_ref, modc_ref).astype(BF16)
        halo = jnp.zeros((SUBLANES, SHIFT_COLS), F32)
        for c0 in range(0, RWKV_COLS, SHIFT_COLS):
            cs = slice(c0, c0 + SHIFT_COLS)
            pe_ref[0:SUBLANES, cs] = halo
            pe_ref[SUBLANES:SUBLANES + CTX_LEN, cs] = _dot(hc, w_ref[:, cs])
            pe_ref[SUBLANES + CTX_LEN:2 * SUBLANES + CTX_LEN, cs] = halo
            shift_cols(CTX_LEN, cs)


def _inproj(x, ctx, mod3, g1, w_rf, mu_prev, mu_next):
    halo_blocks = MM_TILE // SUBLANES
    last_halo = SEQ // SUBLANES - 1
    lat = lambda j: jnp.minimum(j, N_LAT_STEPS - 1)
    return pl.pallas_call(
        _inproj_kernel,
        out_shape=(jax.ShapeDtypeStruct((BATCH, SEQ_ALL, RWKV_COLS), F32),
                   jax.ShapeDtypeStruct((BATCH, SEQ, FOURIER_WIDTH), BF16)),
        grid=(BATCH, N_LAT_STEPS + 1),
        in_specs=[
            pl.BlockSpec((1, MM_TILE, D_MODEL), lambda b, j: (b, lat(j), 0)),
            pl.BlockSpec((1, SUBLANES, D_MODEL), lambda b, j: (b, jnp.maximum(lat(j) * halo_blocks - 1, 0), 0)),
            pl.BlockSpec((1, SUBLANES, D_MODEL),
                         lambda b, j: (b, jnp.minimum((lat(j) + 1) * halo_blocks, last_halo), 0)),
            pl.BlockSpec((1, CTX_LEN, D_MODEL), lambda b, j: (b, 0, 0)),
            pl.BlockSpec((1, 1, 2 * D_MODEL), lambda b, j: (b, 0, 0)),
            pl.BlockSpec((1, 1, 2 * D_MODEL), lambda b, j: (BATCH, 0, 0)),
            _const_spec((1, D_MODEL)),
            _const_spec((D_MODEL, GATE_START)),
            _const_spec((1, RWKV_COLS)),
            _const_spec((1, RWKV_COLS)),
        ],
        out_specs=(
            pl.BlockSpec((1, MM_TILE, RWKV_COLS), lambda b, j: (b, j, 0)),
            pl.BlockSpec((1, MM_TILE, FOURIER_WIDTH), lambda b, j: (b, lat(j), 0)),
        ),
        scratch_shapes=[pltpu.VMEM((MM_TILE + 2 * SUBLANES, RWKV_COLS), F32)],
        compiler_params=pltpu.CompilerParams(
            dimension_semantics=("arbitrary", "arbitrary"), vmem_limit_bytes=VMEM_LIMIT),
        name="inproj_shift",
    )(x, x, x, ctx, mod3, mod3, g1, w_rf, mu_prev, mu_next)


_S_R, _S_V, _S_KD, _S_AL, _S_BE, _S_LD, _S_LGI = range(7)
_N_PLANES = 7
_V_W0F, _V_W0B, _V_A0F, _V_A0B, _V_KK, _V_KA, _V_RK = range(7)


def _wkv_kernel(pf_ref, pb_ref, wlo_ref, alo_ref, vec_ref, eh_ref, esel_ref, tri_ref,
                yf_ref, yb_ref, bsf_ref, bsb_ref, zt_ref, s_ref):
    j = pl.program_id(1)

    @pl.when(j == 0)
    def _():
        zt_ref[...] = jnp.zeros_like(zt_ref)

    k_k = vec_ref[_V_KK:_V_KK + 1, :]
    k_a = vec_ref[_V_KA:_V_KA + 1, :]
    r_k = vec_ref[_V_RK:_V_RK + 1, :]

    for d, (p_ref, bs_ref) in enumerate(((pf_ref, bsf_ref), (pb_ref, bsb_ref))):
        w0 = vec_ref[_V_W0F + d:_V_W0F + d + 1, :]
        a0 = vec_ref[_V_A0F + d:_V_A0F + d + 1, :]
        r = p_ref[0, :, COL_R:COL_R + RWKV_WIDTH]
        k = p_ref[0, :, COL_K:COL_K + RWKV_WIDTH]
        v = p_ref[0, :, COL_V:COL_V + RWKV_WIDTH]
        wa = p_ref[0, :, COL_WD:COL_WD + LANES]
        aa = p_ref[0, :, COL_AD:COL_AD + LANES]
        kq = k * k_k
        ss = _dot((kq * kq).astype(BF16), eh_ref[...])
        kk = kq * lax.rsqrt(jnp.maximum(ss, 1e-24))
        wl = w0 + _dot(jnp.tanh(wa).astype(BF16), wlo_ref[d])
        ld = -DECAY_SCALE * _sigmoid(wl)
        a = _sigmoid(a0 + _dot(aa.astype(BF16), alo_ref[d]))
        kd = k * (1.0 + (a - 1.0) * k_a)
        base = d * _N_PLANES
        s_ref[base + _S_R] = r
        s_ref[base + _S_V] = v
        s_ref[base + _S_KD] = kd
        s_ref[base + _S_AL] = -kk
        s_ref[base + _S_BE] = kk * a
        s_ref[base + _S_LD] = ld
        s_ref[base + _S_LGI] = _dot_split3(tri_ref[d], ld)
        bs_ref[0] = _dot((r * kd * r_k).astype(BF16), esel_ref[...])

    ri = lax.broadcasted_iota(jnp.int32, (CHUNK, LANES), 0)
    li = lax.broadcasted_iota(jnp.int32, (CHUNK, LANES), 1)
    ci = jnp.bitwise_and(li, HEAD_DIM - 1)
    lane_lo = li < HEAD_DIM
    eye = ri == ci
    strict = (ci < ri, ci > ri)
    incl = (ci <= ri, ci >= ri)
    merge_masks = [jnp.right_shift(ri, 1) == jnp.right_shift(ci, 1)]
    for sh in range(1, CHUNK.bit_length() - 1):
        merge_masks.append(jnp.logical_and(jnp.right_shift(ri, sh + 1) == jnp.right_shift(ci, sh + 1),
                                           jnp.right_shift(ri, sh) != jnp.right_shift(ci, sh)))
    n2 = 2 * CHUNK
    same_head = (lax.broadcasted_iota(jnp.int32, (n2, LANES), 0) >= CHUNK) == \
        (lax.broadcasted_iota(jnp.int32, (n2, LANES), 1) >= HEAD_DIM)

    def stack(u):
        return jnp.concatenate([jnp.where(lane_lo, u, 0.0), jnp.where(lane_lo, 0.0, u)], axis=0)

    def load(d, p, r0):
        base = d * _N_PLANES
        ls = slice(p * LANES, (p + 1) * LANES)
        rows = slice(r0, r0 + CHUNK)
        return tuple(s_ref[base + pln, rows, ls] for pln in (_S_LD, _S_LGI, _S_R, _S_V, _S_KD, _S_AL, _S_BE))

    def phase_a(dirs, ins):
        us = range(len(dirs))
        ld, lgi, r, v, kd, al, be = ([ins[u][f] for u in us] for f in range(7))
        m = [lgi[u][CHUNK // 2:CHUNK // 2 + 1, :] for u in us]
        last = [CHUNK - 1 if dirs[u] == 0 else 0 for u in us]
        lgc = [lgi[u][last[u]:last[u] + 1, :] for u in us]
        lge = [lgi[u] - ld[u] for u in us]
        e_m = [jnp.exp(m[u] - lgi[u]) for u in us]
        lhs = [jnp.concatenate([al[u] * jnp.exp(lge[u] - m[u]), r[u] * jnp.exp(lgi[u] - m[u])], axis=0).astype(BF16)
               for u in us]
        rhs = [jnp.concatenate([stack(be[u] * e_m[u]), stack(kd[u] * e_m[u])], axis=0).astype(BF16) for u in us]
        sc = [_dot_nt(lhs[u], rhs[u]) for u in us]
        l_ab = [jnp.where(strict[dirs[u]], sc[u][0:CHUNK, 0:LANES], 0.0) for u in us]

        t_m = [jnp.where(eye, 1.0, jnp.where(merge_masks[0], l_ab[u], 0.0)) for u in us]
        for mk in merge_masks[1:]:
            t_bd = [stack(t_m[u]).astype(BF16) for u in us]
            e_l = [_dot(jnp.where(mk, l_ab[u], 0.0).astype(BF16), t_bd[u]) for u in us]
            t_m = [t_m[u] + _dot(t_m[u].astype(BF16), stack(e_l[u]).astype(BF16)) for u in us]
        t_b = [t_m[u].astype(BF16) for u in us]

        v2 = [stack(v[u]).astype(BF16) for u in us]
        nv = [_dot(jnp.concatenate([jnp.where(strict[dirs[u]], sc[u][0:CHUNK, LANES:2 * LANES], 0.0),
                                    jnp.where(incl[dirs[u]], sc[u][CHUNK:n2, LANES:2 * LANES], 0.0)],
                                   axis=0).astype(BF16), v2[u]) for u in us]
        wu = [_dot(t_b[u], jnp.concatenate([stack(al[u] * jnp.exp(lge[u])), stack(nv[u][0:CHUNK])],
                                           axis=1).astype(BF16)) for u in us]
        m_rb = [jnp.where(incl[dirs[u]], sc[u][CHUNK:n2, 0:LANES], 0.0).astype(BF16) for u in us]
        qy = [_dot(m_rb[u], jnp.concatenate([stack(wu[u][:, 0:LANES]), stack(wu[u][:, LANES:2 * LANES])],
                                            axis=1).astype(BF16)) for u in us]
        q = [(r[u] * jnp.exp(lgi[u]) + qy[u][:, 0:LANES]).astype(BF16) for u in us]
        y0 = [nv[u][CHUNK:n2] + qy[u][:, LANES:2 * LANES] for u in us]
        e_c = [jnp.exp(lgc[u] - lgi[u]) for u in us]
        b_e = [(be[u] * e_c[u]).astype(BF16) for u in us]
        g_l = [jnp.where(same_head, _dot_tn(wu[u][:, 0:LANES].astype(BF16), b_e[u]), 0.0).astype(BF16)
               for u in us]
        h_f = [_dot_tn(jnp.concatenate([wu[u][:, LANES:2 * LANES], v[u]], axis=0).astype(BF16),
                       jnp.concatenate([b_e[u], (kd[u] * e_c[u]).astype(BF16)], axis=0)) for u in us]
        h_t = [jnp.where(lane_lo, h_f[u][0:CHUNK], h_f[u][CHUNK:n2]) for u in us]
        g_c = [jnp.exp(lgc[u]) for u in us]
        return q, y0, g_l, h_t, g_c

    n_chunks = TOK_TILE // CHUNK
    pairs = HEADS // 2
    units = [(d, p, i if d == 0 else n_chunks - 1 - i)
             for i in range(n_chunks) for d in range(2) for p in range(pairs)]
    ins = [load(d, p, c * CHUNK) for d, p, c in units]
    q, y0, g_l, h_t, g_c = phase_a([d for d, _, _ in units], ins)
    y_refs = (yf_ref, yb_ref)
    z = {(d, p): zt_ref[d, p] for d in range(2) for p in range(pairs)}
    for u, (d, p, c) in enumerate(units):
        y_refs[d][0, c * CHUNK:(c + 1) * CHUNK, p * LANES:(p + 1) * LANES] = \
            _dot_nt(q[u], stack(z[d, p]).astype(BF16)) + y0[u]
        z[d, p] = z[d, p] * g_c[u] + _dot(z[d, p].astype(BF16), g_l[u]) + h_t[u]
    for (d, p), val in z.items():
        zt_ref[d, p] = val


def _wkv(p_all, wlo, alo, vecs, eh, esel, tri):
    def fwd_tile(j):
        return jnp.where(j == 0, N_LAT_TILES, j - 1)

    def bwd_tile(j):
        return jnp.where(j == 0, N_LAT_TILES, N_LAT_TILES - j)

    def bwd_out(j):
        return jnp.where(j == 0, N_LAT_TILES - 1, N_LAT_TILES - j)

    y_shape = jax.ShapeDtypeStruct((BATCH, SEQ, RWKV_WIDTH), F32)
    bs_shape = jax.ShapeDtypeStruct((BATCH, SEQ, LANES), F32)
    return pl.pallas_call(
        _wkv_kernel,
        out_shape=(y_shape, y_shape, bs_shape, bs_shape),
        grid=(BATCH, N_TILES),
        in_specs=[
            pl.BlockSpec((1, TOK_TILE, RWKV_COLS), lambda b, j: (b, fwd_tile(j), 0)),
            pl.BlockSpec((1, TOK_TILE, RWKV_COLS), lambda b, j: (b, bwd_tile(j), 0)),
            _const_spec((2, LANES, RWKV_WIDTH)),
            _const_spec((2, LANES, RWKV_WIDTH)),
            _const_spec((SUBLANES, RWKV_WIDTH)),
            _const_spec((RWKV_WIDTH, RWKV_WIDTH)),
            _const_spec((RWKV_WIDTH, LANES)),
            _const_spec((2, TOK_TILE, TOK_TILE)),
        ],
        out_specs=(
            pl.BlockSpec((1, TOK_TILE, RWKV_WIDTH), lambda b, j: (b, jnp.maximum(j - 1, 0), 0)),
            pl.BlockSpec((1, TOK_TILE, RWKV_WIDTH), lambda b, j: (b, bwd_out(j), 0)),
            pl.BlockSpec((1, TOK_TILE, LANES), lambda b, j: (b, jnp.maximum(j - 1, 0), 0)),
            pl.BlockSpec((1, TOK_TILE, LANES), lambda b, j: (b, bwd_out(j), 0)),
        ),
        scratch_shapes=[
            pltpu.VMEM((2, HEADS // 2, HEAD_DIM, LANES), F32),
            pltpu.VMEM((2 * _N_PLANES, TOK_TILE, RWKV_WIDTH), F32),
        ],
        compiler_params=pltpu.CompilerParams(
            dimension_semantics=("arbitrary", "arbitrary"), vmem_limit_bytes=VMEM_LIMIT),
        name="wkv7_chunked",
    )(p_all, p_all, wlo, alo, vecs, eh, esel, tri)


GRID_H = SEQ // GRID_W
HALF_ROWS = GRID_H // 2 + 1
HALF_TOK = HALF_ROWS * GRID_W
MIRROR_TOK = SEQ - HALF_TOK


def _fourier_kernel(x_ref, cd_ref, ab_ref, jc_ref, jj_ref, o_ref, rhs_ref, m_ref):
    for g in range(FOURIER_GROUPS):
        gs = slice(g * FOURIER_GROUP_DIM, (g + 1) * FOURIER_GROUP_DIM)
        z = _dot(x_ref[0, :, gs], cd_ref[...])
        rhs_ref[0:SEQ, gs] = z[:, 0:FOURIER_GROUP_DIM].astype(BF16)
        rhs_ref[SEQ:2 * SEQ, gs] = z[:, FOURIER_GROUP_DIM:2 * FOURIER_GROUP_DIM].astype(BF16)
    yh = _dot(ab_ref[...], rhs_ref[...]).astype(BF16)
    o_ref[0, 0:HALF_TOK, :] = yh
    for k in range(GRID_H - HALF_ROWS):
        src = GRID_H - HALF_ROWS - k
        m_ref[k * GRID_W:(k + 1) * GRID_W, :] = _dot(
            jc_ref[...], yh[src * GRID_W:(src + 1) * GRID_W, :]).astype(BF16)
    o_ref[0, HALF_TOK:SEQ, :] = _dot(m_ref[...], jj_ref[...]).astype(BF16)


def _fourier(xf, cd, ab, jc, jj):
    return pl.pallas_call(
        _fourier_kernel,
        out_shape=jax.ShapeDtypeStruct((BATCH, SEQ, FOURIER_WIDTH), BF16),
        grid=(BATCH,),
        in_specs=[
            pl.BlockSpec((1, SEQ, FOURIER_WIDTH), lambda b: (b, 0, 0)),
            _const_spec((FOURIER_GROUP_DIM, 2 * FOURIER_GROUP_DIM)),
            _const_spec((HALF_TOK, 2 * SEQ)),
            _const_spec((GRID_W, GRID_W)),
            _const_spec((FOURIER_WIDTH, FOURIER_WIDTH)),
        ],
        out_specs=pl.BlockSpec((1, SEQ, FOURIER_WIDTH), lambda b: (b, 0, 0)),
        scratch_shapes=[pltpu.VMEM((2 * SEQ, FOURIER_WIDTH), BF16),
                        pltpu.VMEM((MIRROR_TOK, FOURIER_WIDTH), BF16)],
        compiler_params=pltpu.CompilerParams(vmem_limit_bytes=VMEM_LIMIT),
        name="fourier_dft",
    )(xf, cd, ab, jc, jj)


def _merge_kernel(x_ref, mod_ref, yf_ref, yb_ref, bsf_ref, bsb_ref, v_ref, gd_ref, fo_ref,
                  g1_ref, wg_ref, g2_ref, wur_ref, wuf_ref, wo_ref, ehm_ref, ex_ref, lng_ref, lnb_ref, o_ref):
    x = x_ref[0]
    shift = mod_ref[0, :, 0:D_MODEL]
    scale = mod_ref[0, :, D_MODEL:2 * D_MODEL]
    gate1 = mod_ref[0, :, 2 * D_MODEL:3 * D_MODEL]
    hx = ((_rms(x) * g1_ref[...]) * (1.0 + scale) + shift).astype(BF16)
    gates = _dot(hx, wg_ref[...])

    y = yf_ref[0] + yb_ref[0]
    mean = _dot(y.astype(BF16), ehm_ref[...])
    dy = y - mean
    var = _dot((dy * dy).astype(BF16), ehm_ref[...])
    o = dy * lax.rsqrt(var + GN_EPS) * lng_ref[...] + lnb_ref[...]
    bonus = _dot((bsf_ref[0] + bsb_ref[0]).astype(BF16), ex_ref[...]) * v_ref[0]
    g = _dot(_sigmoid(gd_ref[0]).astype(BF16), g2_ref[...])
    o = ((o + bonus) * g).astype(BF16)
    r_up = _dot(o, wur_ref[...])
    f_up = _dot(fo_ref[0], wuf_ref[...])
    mix = (_sigmoid(gates[:, 0:D_MODEL]) * f_up + _sigmoid(gates[:, D_MODEL:2 * D_MODEL]) * r_up).astype(BF16)
    o_ref[0] = x + gate1 * _dot(mix, wo_ref[...])


def _merge(x, mod3, yf, yb, bsf, bsb, p_all, fo, g1, wg, g2, wur, wuf, wo, ehm, ex, lng, lnb):
    tok = lambda w: pl.BlockSpec((1, MM_TILE, w), lambda b, t: (b, t, 0))
    return pl.pallas_call(
        _merge_kernel,
        out_shape=jax.ShapeDtypeStruct((BATCH, SEQ, D_MODEL), F32),
        grid=(BATCH, SEQ // MM_TILE),
        in_specs=[
            tok(D_MODEL),
            pl.BlockSpec((1, 1, 6 * D_MODEL), lambda b, t: (b, 0, 0)),
            tok(RWKV_WIDTH), tok(RWKV_WIDTH), tok(LANES), tok(LANES),
            pl.BlockSpec((1, MM_TILE, RWKV_WIDTH), lambda b, t: (b, t, COL_V // RWKV_WIDTH)),
            pl.BlockSpec((1, MM_TILE, GATE_LORA), lambda b, t: (b, t, COL_GD // GATE_LORA)),
            tok(FOURIER_WIDTH),
            _const_spec((1, D_MODEL)),
            _const_spec((D_MODEL, 2 * D_MODEL)),
            _const_spec((GATE_LORA, RWKV_WIDTH)),
            _const_spec((RWKV_WIDTH, D_MODEL)),
            _const_spec((FOURIER_WIDTH, D_MODEL)),
            _const_spec((D_MODEL, D_MODEL)),
            _const_spec((RWKV_WIDTH, RWKV_WIDTH)),
            _const_spec((LANES, RWKV_WIDTH)),
            _const_spec((1, RWKV_WIDTH)),
            _const_spec((1, RWKV_WIDTH)),
        ],
        out_specs=tok(D_MODEL),
        compiler_params=pltpu.CompilerParams(
            dimension_semantics=("arbitrary", "arbitrary"), vmem_limit_bytes=VMEM_LIMIT),
        name="branch_merge",
    )(x, mod3, yf, yb, bsf, bsb, p_all, p_all, fo, g1, wg, g2, wur, wuf, wo, ehm, ex, lng, lnb)


def _ffn_kernel(x_ref, mod_ref, g2_ref, wgu_ref, wd_ref, gf_ref, o_ref):
    x = x_ref[0]
    shift = mod_ref[0, :, 3 * D_MODEL:4 * D_MODEL]
    scale = mod_ref[0, :, 4 * D_MODEL:5 * D_MODEL]
    gate2 = mod_ref[0, :, 5 * D_MODEL:6 * D_MODEL]
    hx = ((_rms(x) * g2_ref[...]) * (1.0 + scale) + shift).astype(BF16)
    part = D_FF // FFN_SPLIT
    acc = jnp.zeros((FFN_TILE, D_MODEL), F32)
    for s in range(FFN_SPLIT):
        gt = _dot(hx, wgu_ref[:, s * part:(s + 1) * part])
        up = _dot(hx, wgu_ref[:, D_FF + s * part:D_FF + (s + 1) * part])
        h = (gt * _sigmoid(gt) * up).astype(BF16)
        acc = acc + _dot(h, wd_ref[s * part:(s + 1) * part, :])
    o_ref[0] = _rms(x + gate2 * acc) * gf_ref[...]


def _ffn(x1, mod3, g2, wgu, wd, gf):
    tiles = SEQ // FFN_TILE
    return pl.pallas_call(
        _ffn_kernel,
        out_shape=jax.ShapeDtypeStruct((BATCH, SEQ, D_MODEL), F32),
        grid=(BATCH, tiles),
        in_specs=[
            pl.BlockSpec((1, FFN_TILE, D_MODEL), lambda b, t: (b, t, 0)),
            pl.BlockSpec((1, 1, 6 * D_MODEL), lambda b, t: (b, 0, 0)),
            _const_spec((1, D_MODEL)),
            _const_spec((D_MODEL, 2 * D_FF)),
            _const_spec((D_FF, D_MODEL)),
            _const_spec((1, D_MODEL)),
        ],
        out_specs=pl.BlockSpec((1, FFN_TILE, D_MODEL), lambda b, t: (b, t, 0)),
        compiler_params=pltpu.CompilerParams(
            dimension_semantics=("arbitrary", "arbitrary"), vmem_limit_bytes=VMEM_LIMIT),
        name="swiglu_final",
    )(x1, mod3, g2, wgu, wd, gf)


@functools.lru_cache(maxsize=None)
def _constants():
    def dft(n):
        ang = 2.0 * np.pi * (np.outer(np.arange(n), np.arange(n)) % n) / n
        return np.cos(ang), -np.sin(ang)

    a_ch, b_ch = dft(FOURIER_GROUP_DIM)
    a_col, b_col = dft(GRID_W)
    a_row, b_row = dft(GRID_H)
    norm = 1.0 / np.sqrt(GRID_H * GRID_W * FOURIER_GROUP_DIM)
    cd = np.concatenate([a_ch, b_ch], axis=1) * norm
    a_tok = np.kron(a_row, a_col) - np.kron(b_row, b_col)
    b_tok = np.kron(a_row, b_col) + np.kron(b_row, a_col)
    ab = np.concatenate([a_tok, -b_tok], axis=1)[0:HALF_TOK]
    neg = lambda n: (np.arange(n)[:, None] == (-np.arange(n)[None, :]) % n).astype(np.float32)
    jc = neg(GRID_W)
    jj = np.kron(np.eye(FOURIER_GROUPS), neg(FOURIER_GROUP_DIM))
    head = np.arange(RWKV_WIDTH) // HEAD_DIM
    eh = (head[:, None] == head[None, :]).astype(np.float32)
    esel = (head[:, None] == np.arange(LANES)[None, :]).astype(np.float32)
    i = np.arange(TOK_TILE)
    same_chunk = (i[:, None] // CHUNK) == (i[None, :] // CHUNK)
    tri = np.stack([same_chunk & (i[None, :] <= i[:, None]), same_chunk & (i[None, :] >= i[:, None])])
    f32 = lambda u: np.asarray(u, np.float32)
    return dict(cd=f32(cd), ab=f32(ab), jc=f32(jc), jj=f32(jj), eh=f32(eh), ehm=f32(eh / HEAD_DIM), esel=f32(esel), ex=f32(esel.T),
                tri=f32(tri))


def kernel(x, c, ctx, c_ctx, norm1_g, norm2_g, w_ada, b_ada, w_in, mu_prev, mu_next, w0_f, w2_f, a0_f, a2_f, w0_b, w2_b, a0_b, a2_b, g2, k_k, k_a, r_k, lnx_g, lnx_b, w_up_r, w_up_f, w_out, w_gu, w_down, final_norm_g):
    cst = {name: jnp.asarray(val).astype(BF16) for name, val in _constants().items()}
    row = lambda u: u.reshape(1, -1)
    cc = jnp.concatenate(
        [c, c_ctx[None, :], jnp.zeros((MOD_ROWS - BATCH - 1, D_MODEL), F32)], axis=0)
    mod = _modulation(cc, w_ada[0].astype(BF16), row(b_ada[0]))
    mod3 = mod.reshape(MOD_ROWS, 1, 6 * D_MODEL)

    w_in0 = w_in[0].astype(BF16)
    p_all, xf = _inproj(x, ctx, mod3, row(norm1_g[0]), w_in0[:, 0:GATE_START], row(mu_prev[0]), row(mu_next[0]))

    zeros_lora = jnp.zeros((DECAY_LORA, RWKV_WIDTH), F32)
    wlo = jnp.stack([jnp.concatenate([w2_f[0], zeros_lora], 0), jnp.concatenate([zeros_lora, w2_b[0]], 0)]).astype(BF16)
    alo = jnp.stack([jnp.concatenate([a2_f[0], zeros_lora], 0), jnp.concatenate([zeros_lora, a2_b[0]], 0)]).astype(BF16)
    vecs = jnp.stack([w0_f[0], w0_b[0], a0_f[0], a0_b[0], k_k[0], k_a[0], r_k[0].reshape(-1),
                      jnp.zeros((RWKV_WIDTH,), F32)])
    yf, yb, bsf, bsb = _wkv(p_all, wlo, alo, vecs, cst["eh"], cst["esel"], cst["tri"])

    fo = _fourier(xf, cst["cd"], cst["ab"], cst["jc"], cst["jj"])

    x1 = _merge(x, mod3, yf, yb, bsf, bsb, p_all, fo, row(norm1_g[0]), w_in0[:, GATE_START:],
                g2[0].astype(BF16), w_up_r[0].astype(BF16), w_up_f[0].astype(BF16), w_out[0].astype(BF16),
                cst["ehm"], cst["ex"], row(lnx_g[0]), row(lnx_b[0]))
    return _ffn(x1, mod3, row(norm2_g[0]), w_gu[0].astype(BF16), w_down[0].astype(BF16), row(final_norm_g))
```

```python
import functools

import jax
import jax.numpy as jnp
import numpy as np
from jax import lax
from jax.experimental import pallas as pl
from jax.experimental.pallas import tpu as pltpu

D_MODEL = 1024
BATCH = 16
SEQ = 2048
GRID_W = 64
CTX_LEN = 256
FOURIER_WIDTH = 512
FOURIER_GROUPS = 4
FOURIER_GROUP_DIM = FOURIER_WIDTH // FOURIER_GROUPS
RWKV_WIDTH = 512
HEAD_DIM = 64
HEADS = RWKV_WIDTH // HEAD_DIM
DECAY_LORA = 64
AAA_LORA = 64
GATE_LORA = 128
RWKV_COLS = 3 * RWKV_WIDTH + 2 * DECAY_LORA + 2 * AAA_LORA + GATE_LORA
FOURIER_START = RWKV_COLS
GATE_START = RWKV_COLS + FOURIER_WIDTH
D_FF = 2816
NORM_EPS = 1e-6
GN_EPS = 64e-5
DECAY_SCALE = float(np.exp(-0.5))

COL_R, COL_K, COL_V = 0, RWKV_WIDTH, 2 * RWKV_WIDTH
COL_WD = 3 * RWKV_WIDTH
COL_AD = COL_WD + 2 * DECAY_LORA
COL_GD = COL_AD + 2 * AAA_LORA
COL_KK = RWKV_COLS
SLAB_COLS = RWKV_COLS + RWKV_WIDTH

LANES = 128
SUBLANES = 8
TOK_TILE = 256
MM_TILE = 512
SHIFT_COLS = 1920
CHUNK = 64
SEQ_ALL = SEQ + CTX_LEN
N_TILES = SEQ_ALL // TOK_TILE
N_LAT_TILES = SEQ // TOK_TILE
FFN_TILE = 512
FFN_SPLIT = 11
MOD_ROWS = 24
VMEM_LIMIT = 56 * 1024 * 1024

F32 = jnp.float32
BF16 = jnp.bfloat16


def _dot(a, b):
    return jnp.dot(a, b, preferred_element_type=F32)


def _dot_nt(a, b):
    return lax.dot_general(a, b, (((1,), (1,)), ((), ())), preferred_element_type=F32)


def _dot_tn(a, b):
    return lax.dot_general(a, b, (((0,), (0,)), ((), ())), preferred_element_type=F32)


def _dot_hilo(x, e):
    hi = x.astype(BF16)
    lo = (x - hi.astype(F32)).astype(BF16)
    return _dot(hi, e) + _dot(lo, e)


def _dot_split3(e, x):
    x1 = x.astype(BF16)
    rem = x - x1.astype(F32)
    x2 = rem.astype(BF16)
    x3 = (rem - x2.astype(F32)).astype(BF16)
    return _dot(e, x1) + _dot(e, x2) + _dot(e, x3)


def _rms(u):
    return u * lax.rsqrt(jnp.mean(u * u, axis=-1, keepdims=True) + NORM_EPS)


def _sigmoid(z):
    return 1.0 / (1.0 + jnp.exp(-z))


def _softplus(z):
    return jnp.maximum(z, 0.0) + jnp.log(1.0 + jnp.exp(-jnp.abs(z)))


def _const_spec(shape):
    nd = len(shape)
    return pl.BlockSpec(shape, lambda *_: (0,) * nd, pipeline_mode=pl.Buffered(1))


def _mod_kernel(c_ref, w_ref, b_ref, o_ref):
    c = c_ref[...]
    s = (c * _sigmoid(c)).astype(BF16)
    o_ref[...] = _dot(s, w_ref[...]) + b_ref[...]


def _modulation(cc, w_ada, b_ada):
    n_blk = 4
    bn = (6 * D_MODEL) // n_blk
    return pl.pallas_call(
        _mod_kernel,
        out_shape=jax.ShapeDtypeStruct((MOD_ROWS, 6 * D_MODEL), F32),
        grid=(n_blk,),
        in_specs=[
            pl.BlockSpec((MOD_ROWS, D_MODEL), lambda n: (0, 0)),
            pl.BlockSpec((D_MODEL, bn), lambda n: (0, n)),
            pl.BlockSpec((1, bn), lambda n: (0, n)),
        ],
        out_specs=pl.BlockSpec((MOD_ROWS, bn), lambda n: (0, n)),
        name="adaln_mod",
    )(cc, w_ada, b_ada)


def _norm_mod(u, g1_ref, mod_ref):
    shift = mod_ref[0, :, 0:D_MODEL]
    scale = mod_ref[0, :, D_MODEL:2 * D_MODEL]
    return (_rms(u) * g1_ref[...]) * (1.0 + scale) + shift


N_LAT_STEPS = SEQ // MM_TILE


def _inproj_kernel(x_ref, xp_ref, xn_ref, c_ref, modb_ref, modc_ref, g1_ref, w_ref, mup_ref, mun_ref, kk_ref, eh_ref,
                   p_ref, xf_ref, pe_ref):
    j = pl.program_id(1)

    def shift_cols(rows, cs):
        pm = pe_ref[pl.ds(SUBLANES, rows), cs]
        pu = pe_ref[pl.ds(SUBLANES - 1, rows), cs]
        pd = pe_ref[pl.ds(SUBLANES + 1, rows), cs]
        p_ref[0, 0:rows, cs] = pm + mup_ref[:, cs] * (pu - pm) + mun_ref[:, cs] * (pd - pm)

    def normalised_keys(rows):
        kq = p_ref[0, 0:rows, COL_K:COL_K + RWKV_WIDTH] * kk_ref[...]
        ss = _dot((kq * kq).astype(BF16), eh_ref[...])
        p_ref[0, 0:rows, COL_KK:COL_KK + RWKV_WIDTH] = kq * lax.rsqrt(jnp.maximum(ss, 1e-24))

    @pl.when(j < N_LAT_STEPS)
    def _():
        prev_ok = jnp.where(j >= 1, 1.0, 0.0)
        next_ok = jnp.where(j <= N_LAT_STEPS - 2, 1.0, 0.0)
        h_ext = jnp.concatenate(
            [_norm_mod(xp_ref[0], g1_ref, modb_ref) * prev_ok, _norm_mod(x_ref[0], g1_ref, modb_ref),
             _norm_mod(xn_ref[0], g1_ref, modb_ref) * next_ok], axis=0).astype(BF16)
        xf_ref[0] = _dot(h_ext[SUBLANES:SUBLANES + MM_TILE], w_ref[:, FOURIER_START:GATE_START]).astype(BF16)
        for c0 in range(0, RWKV_COLS, SHIFT_COLS):
            cs = slice(c0, c0 + SHIFT_COLS)
            pe_ref[:, cs] = _dot(h_ext, w_ref[:, cs])
            shift_cols(MM_TILE, cs)
        normalised_keys(MM_TILE)

    @pl.when(j == N_LAT_STEPS)
    def _():
        hc = _norm_mod(c_ref[0], g1_ref, modc_ref).astype(BF16)
        halo = jnp.zeros((SUBLANES, SHIFT_COLS), F32)
        for c0 in range(0, RWKV_COLS, SHIFT_COLS):
            cs = slice(c0, c0 + SHIFT_COLS)
            pe_ref[0:SUBLANES, cs] = halo
            pe_ref[SUBLANES:SUBLANES + CTX_LEN, cs] = _dot(hc, w_ref[:, cs])
            pe_ref[SUBLANES + CTX_LEN:2 * SUBLANES + CTX_LEN, cs] = halo
            shift_cols(CTX_LEN, cs)
        normalised_keys(CTX_LEN)


def _inproj(x, ctx, mod3, g1, w_rf, mu_prev, mu_next, k_k, eh):
    halo_blocks = MM_TILE // SUBLANES
    last_halo = SEQ // SUBLANES - 1
    lat = lambda j: jnp.minimum(j, N_LAT_STEPS - 1)
    return pl.pallas_call(
        _inproj_kernel,
        out_shape=(jax.ShapeDtypeStruct((BATCH, SEQ_ALL, SLAB_COLS), F32),
                   jax.ShapeDtypeStruct((BATCH, SEQ, FOURIER_WIDTH), BF16)),
        grid=(BATCH, N_LAT_STEPS + 1),
        in_specs=[
            pl.BlockSpec((1, MM_TILE, D_MODEL), lambda b, j: (b, lat(j), 0)),
            pl.BlockSpec((1, SUBLANES, D_MODEL), lambda b, j: (b, jnp.maximum(lat(j) * halo_blocks - 1, 0), 0)),
            pl.BlockSpec((1, SUBLANES, D_MODEL),
                         lambda b, j: (b, jnp.minimum((lat(j) + 1) * halo_blocks, last_halo), 0)),
            pl.BlockSpec((1, CTX_LEN, D_MODEL), lambda b, j: (b, 0, 0)),
            pl.BlockSpec((1, 1, 2 * D_MODEL), lambda b, j: (b, 0, 0)),
            pl.BlockSpec((1, 1, 2 * D_MODEL), lambda b, j: (BATCH, 0, 0)),
            _const_spec((1, D_MODEL)),
            _const_spec((D_MODEL, GATE_START)),
            _const_spec((1, RWKV_COLS)),
            _const_spec((1, RWKV_COLS)),
            _const_spec((1, RWKV_WIDTH)),
            _const_spec((RWKV_WIDTH, RWKV_WIDTH)),
        ],
        out_specs=(
            pl.BlockSpec((1, MM_TILE, SLAB_COLS), lambda b, j: (b, j, 0)),
            pl.BlockSpec((1, MM_TILE, FOURIER_WIDTH), lambda b, j: (b, lat(j), 0)),
        ),
        scratch_shapes=[pltpu.VMEM((MM_TILE + 2 * SUBLANES, RWKV_COLS), F32)],
        compiler_params=pltpu.CompilerParams(
            dimension_semantics=("arbitrary", "arbitrary"), vmem_limit_bytes=VMEM_LIMIT),
        name="inproj_shift",
    )(x, x, x, ctx, mod3, mod3, g1, w_rf, mu_prev, mu_next, k_k, eh)


_S_KD, _S_BE, _S_LD, _S_LGI = range(4)
_N_PLANES = 4
_V_W0F, _V_W0B, _V_A0F, _V_A0B, _V_KA, _V_RK = range(6)


def _wkv_kernel(pf_ref, pb_ref, wlo_ref, alo_ref, vec_ref, esel_ref, tri_ref,
                yf_ref, yb_ref, bsf_ref, bsb_ref, zt_ref, s_ref):
    j = pl.program_id(1)

    @pl.when(j == 0)
    def _():
        zt_ref[...] = jnp.zeros_like(zt_ref)

    k_a = vec_ref[_V_KA:_V_KA + 1, :]
    r_k = vec_ref[_V_RK:_V_RK + 1, :]
    p_refs = (pf_ref, pb_ref)

    for d, bs_ref in enumerate((bsf_ref, bsb_ref)):
        p_ref = p_refs[d]
        w0 = vec_ref[_V_W0F + d:_V_W0F + d + 1, :]
        a0 = vec_ref[_V_A0F + d:_V_A0F + d + 1, :]
        k = p_ref[0, :, COL_K:COL_K + RWKV_WIDTH]
        wa = p_ref[0, :, COL_WD:COL_WD + LANES]
        aa = p_ref[0, :, COL_AD:COL_AD + LANES]
        wl = w0 + _dot(jnp.tanh(wa).astype(BF16), wlo_ref[d])
        ld = -DECAY_SCALE * _sigmoid(wl)
        a = _sigmoid(a0 + _dot(aa.astype(BF16), alo_ref[d]))
        kd = k * (1.0 + (a - 1.0) * k_a)
        base = d * _N_PLANES
        s_ref[base + _S_KD] = kd
        s_ref[base + _S_BE] = p_ref[0, :, COL_KK:COL_KK + RWKV_WIDTH] * a
        s_ref[base + _S_LD] = ld
        s_ref[base + _S_LGI] = _dot_split3(tri_ref[d], ld)
        bs_ref[0] = _dot((p_ref[0, :, COL_R:COL_R + RWKV_WIDTH] * kd * r_k).astype(BF16), esel_ref[...])

    ri = lax.broadcasted_iota(jnp.int32, (CHUNK, LANES), 0)
    li = lax.broadcasted_iota(jnp.int32, (CHUNK, LANES), 1)
    ci = jnp.bitwise_and(li, HEAD_DIM - 1)
    lane_lo = li < HEAD_DIM
    eye = ri == ci
    strict = (ci < ri, ci > ri)
    incl = (ci <= ri, ci >= ri)
    merge_masks = [jnp.right_shift(ri, 1) == jnp.right_shift(ci, 1)]
    for sh in range(1, CHUNK.bit_length() - 1):
        merge_masks.append(jnp.logical_and(jnp.right_shift(ri, sh + 1) == jnp.right_shift(ci, sh + 1),
                                           jnp.right_shift(ri, sh) != jnp.right_shift(ci, sh)))
    n2 = 2 * CHUNK
    same_head = (lax.broadcasted_iota(jnp.int32, (n2, LANES), 0) >= CHUNK) == \
        (lax.broadcasted_iota(jnp.int32, (n2, LANES), 1) >= HEAD_DIM)

    def stack(u):
        return jnp.concatenate([jnp.where(lane_lo, u, 0.0), jnp.where(lane_lo, 0.0, u)], axis=0)

    def load(d, p, r0):
        base = d * _N_PLANES
        rows = slice(r0, r0 + CHUNK)
        plane = lambda pln: s_ref[base + pln, rows, p * LANES:(p + 1) * LANES]
        col = lambda c0: p_refs[d][0, rows, c0 + p * LANES:c0 + (p + 1) * LANES]
        return (plane(_S_LD), plane(_S_LGI), col(COL_R), col(COL_V), plane(_S_KD), -col(COL_KK), plane(_S_BE))

    def phase_a(dirs, ins):
        us = range(len(dirs))
        ld, lgi, r, v, kd, al, be = ([ins[u][f] for u in us] for f in range(7))
        m = [lgi[u][CHUNK // 2:CHUNK // 2 + 1, :] for u in us]
        last = [CHUNK - 1 if dirs[u] == 0 else 0 for u in us]
        lgc = [lgi[u][last[u]:last[u] + 1, :] for u in us]
        lge = [lgi[u] - ld[u] for u in us]
        e_m = [jnp.exp(m[u] - lgi[u]) for u in us]
        lhs = [jnp.concatenate([al[u] * jnp.exp(lge[u] - m[u]), r[u] * jnp.exp(lgi[u] - m[u])], axis=0).astype(BF16)
               for u in us]
        rhs = [jnp.concatenate([stack(be[u] * e_m[u]), stack(kd[u] * e_m[u])], axis=0).astype(BF16) for u in us]
        sc = [_dot_nt(lhs[u], rhs[u]) for u in us]
        l_ab = [jnp.where(strict[dirs[u]], sc[u][0:CHUNK, 0:LANES], 0.0) for u in us]

        t_m = [jnp.where(eye, 1.0, jnp.where(merge_masks[0], l_ab[u], 0.0)) for u in us]
        for mk in merge_masks[1:]:
            t_bd = [stack(t_m[u]).astype(BF16) for u in us]
            e_l = [_dot(jnp.where(mk, l_ab[u], 0.0).astype(BF16), t_bd[u]) for u in us]
            t_m = [t_m[u] + _dot(t_m[u].astype(BF16), stack(e_l[u]).astype(BF16)) for u in us]
        t_b = [t_m[u].astype(BF16) for u in us]

        v2 = [stack(v[u]).astype(BF16) for u in us]
        nv = [_dot(jnp.concatenate([jnp.where(strict[dirs[u]], sc[u][0:CHUNK, LANES:2 * LANES], 0.0),
                                    jnp.where(incl[dirs[u]], sc[u][CHUNK:n2, LANES:2 * LANES], 0.0)],
                                   axis=0).astype(BF16), v2[u]) for u in us]
        wu = [_dot(t_b[u], jnp.concatenate([stack(al[u] * jnp.exp(lge[u])), stack(nv[u][0:CHUNK])],
                                           axis=1).astype(BF16)) for u in us]
        m_rb = [jnp.where(incl[dirs[u]], sc[u][CHUNK:n2, 0:LANES], 0.0).astype(BF16) for u in us]
        qy = [_dot(m_rb[u], jnp.concatenate([stack(wu[u][:, 0:LANES]), stack(wu[u][:, LANES:2 * LANES])],
                                            axis=1).astype(BF16)) for u in us]
        q = [(r[u] * jnp.exp(lgi[u]) + qy[u][:, 0:LANES]).astype(BF16) for u in us]
        y0 = [nv[u][CHUNK:n2] + qy[u][:, LANES:2 * LANES] for u in us]
        e_c = [jnp.exp(lgc[u] - lgi[u]) for u in us]
        b_e = [(be[u] * e_c[u]).astype(BF16) for u in us]
        g_l = [jnp.where(same_head, _dot_tn(wu[u][:, 0:LANES].astype(BF16), b_e[u]), 0.0).astype(BF16)
               for u in us]
        h_f = [_dot_tn(jnp.concatenate([wu[u][:, LANES:2 * LANES], v[u]], axis=0).astype(BF16),
                       jnp.concatenate([b_e[u], (kd[u] * e_c[u]).astype(BF16)], axis=0)) for u in us]
        h_t = [jnp.where(lane_lo, h_f[u][0:CHUNK], h_f[u][CHUNK:n2]) for u in us]
        g_c = [jnp.exp(lgc[u]) for u in us]
        return q, y0, g_l, h_t, g_c

    n_chunks = TOK_TILE // CHUNK
    pairs = HEADS // 2
    units = [(d, p, i if d == 0 else n_chunks - 1 - i)
             for i in range(n_chunks) for d in range(2) for p in range(pairs)]
    ins = [load(d, p, c * CHUNK) for d, p, c in units]
    q, y0, g_l, h_t, g_c = phase_a([d for d, _, _ in units], ins)
    y_refs = (yf_ref, yb_ref)
    z = {(d, p): zt_ref[d, p] for d in range(2) for p in range(pairs)}
    for u, (d, p, c) in enumerate(units):
        y_refs[d][0, c * CHUNK:(c + 1) * CHUNK, p * LANES:(p + 1) * LANES] = \
            _dot_nt(q[u], stack(z[d, p]).astype(BF16)) + y0[u]
        z[d, p] = z[d, p] * g_c[u] + _dot(z[d, p].astype(BF16), g_l[u]) + h_t[u]
    for (d, p), val in z.items():
        zt_ref[d, p] = val


def _wkv(p_all, wlo, alo, vecs, esel, tri):
    def fwd_tile(j):
        return jnp.where(j == 0, N_LAT_TILES, j - 1)

    def bwd_tile(j):
        return jnp.where(j == 0, N_LAT_TILES, N_LAT_TILES - j)

    def bwd_out(j):
        return jnp.where(j == 0, N_LAT_TILES - 1, N_LAT_TILES - j)

    y_shape = jax.ShapeDtypeStruct((BATCH, SEQ, RWKV_WIDTH), F32)
    bs_shape = jax.ShapeDtypeStruct((BATCH, SEQ, LANES), F32)
    return pl.pallas_call(
        _wkv_kernel,
        out_shape=(y_shape, y_shape, bs_shape, bs_shape),
        grid=(BATCH, N_TILES),
        in_specs=[
            pl.BlockSpec((1, TOK_TILE, SLAB_COLS), lambda b, j: (b, fwd_tile(j), 0)),
            pl.BlockSpec((1, TOK_TILE, SLAB_COLS), lambda b, j: (b, bwd_tile(j), 0)),
            _const_spec((2, LANES, RWKV_WIDTH)),
            _const_spec((2, LANES, RWKV_WIDTH)),
            _const_spec((SUBLANES, RWKV_WIDTH)),
            _const_spec((RWKV_WIDTH, LANES)),
            _const_spec((2, TOK_TILE, TOK_TILE)),
        ],
        out_specs=(
            pl.BlockSpec((1, TOK_TILE, RWKV_WIDTH), lambda b, j: (b, jnp.maximum(j - 1, 0), 0)),
            pl.BlockSpec((1, TOK_TILE, RWKV_WIDTH), lambda b, j: (b, bwd_out(j), 0)),
            pl.BlockSpec((1, TOK_TILE, LANES), lambda b, j: (b, jnp.maximum(j - 1, 0), 0)),
            pl.BlockSpec((1, TOK_TILE, LANES), lambda b, j: (b, bwd_out(j), 0)),
        ),
        scratch_shapes=[
            pltpu.VMEM((2, HEADS // 2, HEAD_DIM, LANES), F32),
            pltpu.VMEM((2 * _N_PLANES, TOK_TILE, RWKV_WIDTH), F32),
        ],
        compiler_params=pltpu.CompilerParams(
            dimension_semantics=("arbitrary", "arbitrary"), vmem_limit_bytes=VMEM_LIMIT),
        name="wkv7_chunked",
    )(p_all, p_all, wlo, alo, vecs, esel, tri)


GRID_H = SEQ // GRID_W
HALF_ROWS = GRID_H // 2 + 1
HALF_TOK = HALF_ROWS * GRID_W
MIRROR_TOK = SEQ - HALF_TOK


def _fourier_kernel(x_ref, cd_ref, ab_ref, jc_ref, jj_ref, o_ref, rhs_ref, m_ref):
    for g in range(FOURIER_GROUPS):
        gs = slice(g * FOURIER_GROUP_DIM, (g + 1) * FOURIER_GROUP_DIM)
        z = _dot(x_ref[0, :, gs], cd_ref[...])
        rhs_ref[0:SEQ, gs] = z[:, 0:FOURIER_GROUP_DIM].astype(BF16)
        rhs_ref[SEQ:2 * SEQ, gs] = z[:, FOURIER_GROUP_DIM:2 * FOURIER_GROUP_DIM].astype(BF16)
    yh = _dot(ab_ref[...], rhs_ref[...]).astype(BF16)
    o_ref[0, 0:HALF_TOK, :] = yh
    for k in range(GRID_H - HALF_ROWS):
        src = GRID_H - HALF_ROWS - k
        m_ref[k * GRID_W:(k + 1) * GRID_W, :] = _dot(
            jc_ref[...], yh[src * GRID_W:(src + 1) * GRID_W, :]).astype(BF16)
    o_ref[0, HALF_TOK:SEQ, :] = _dot(m_ref[...], jj_ref[...]).astype(BF16)


def _fourier(xf, cd, ab, jc, jj):
    return pl.pallas_call(
        _fourier_kernel,
        out_shape=jax.ShapeDtypeStruct((BATCH, SEQ, FOURIER_WIDTH), BF16),
        grid=(BATCH,),
        in_specs=[
            pl.BlockSpec((1, SEQ, FOURIER_WIDTH), lambda b: (b, 0, 0)),
            _const_spec((FOURIER_GROUP_DIM, 2 * FOURIER_GROUP_DIM)),
            _const_spec((HALF_TOK, 2 * SEQ)),
            _const_spec((GRID_W, GRID_W)),
            _const_spec((FOURIER_WIDTH, FOURIER_WIDTH)),
        ],
        out_specs=pl.BlockSpec((1, SEQ, FOURIER_WIDTH), lambda b: (b, 0, 0)),
        scratch_shapes=[pltpu.VMEM((2 * SEQ, FOURIER_WIDTH), BF16),
                        pltpu.VMEM((MIRROR_TOK, FOURIER_WIDTH), BF16)],
        compiler_params=pltpu.CompilerParams(vmem_limit_bytes=VMEM_LIMIT),
        name="fourier_dft",
    )(xf, cd, ab, jc, jj)


def _merge_kernel(x_ref, mod_ref, yf_ref, yb_ref, bsf_ref, bsb_ref, v_ref, gd_ref, fo_ref,
                  g1_ref, wg_ref, g2_ref, wur_ref, wuf_ref, wo_ref, ehm_ref, ex_ref, lng_ref, lnb_ref, o_ref):
    x = x_ref[0]
    shift = mod_ref[0, :, 0:D_MODEL]
    scale = mod_ref[0, :, D_MODEL:2 * D_MODEL]
    gate1 = mod_ref[0, :, 2 * D_MODEL:3 * D_MODEL]
    hx = ((_rms(x) * g1_ref[...]) * (1.0 + scale) + shift).astype(BF16)
    gates = _dot(hx, wg_ref[...])

    y = yf_ref[0] + yb_ref[0]
    mean = _dot(y.astype(BF16), ehm_ref[...])
    dy = y - mean
    var = _dot((dy * dy).astype(BF16), ehm_ref[...])
    o = dy * lax.rsqrt(var + GN_EPS) * lng_ref[...] + lnb_ref[...]
    bonus = _dot((bsf_ref[0] + bsb_ref[0]).astype(BF16), ex_ref[...]) * v_ref[0]
    g = _dot(_sigmoid(gd_ref[0]).astype(BF16), g2_ref[...])
    o = ((o + bonus) * g).astype(BF16)
    r_up = _dot(o, wur_ref[...])
    f_up = _dot(fo_ref[0], wuf_ref[...])
    mix = (_sigmoid(gates[:, 0:D_MODEL]) * f_up + _sigmoid(gates[:, D_MODEL:2 * D_MODEL]) * r_up).astype(BF16)
    o_ref[0] = x + gate1 * _dot(mix, wo_ref[...])


def _merge(x, mod3, yf, yb, bsf, bsb, p_all, fo, g1, wg, g2, wur, wuf, wo, ehm, ex, lng, lnb):
    tok = lambda w: pl.BlockSpec((1, MM_TILE, w), lambda b, t: (b, t, 0))
    return pl.pallas_call(
        _merge_kernel,
        out_shape=jax.ShapeDtypeStruct((BATCH, SEQ, D_MODEL), F32),
        grid=(BATCH, SEQ // MM_TILE),
        in_specs=[
            tok(D_MODEL),
            pl.BlockSpec((1, 1, 6 * D_MODEL), lambda b, t: (b, 0, 0)),
            tok(RWKV_WIDTH), tok(RWKV_WIDTH), tok(LANES), tok(LANES),
            pl.BlockSpec((1, MM_TILE, RWKV_WIDTH), lambda b, t: (b, t, COL_V // RWKV_WIDTH)),
            pl.BlockSpec((1, MM_TILE, GATE_LORA), lambda b, t: (b, t, COL_GD // GATE_LORA)),
            tok(FOURIER_WIDTH),
            _const_spec((1, D_MODEL)),
            _const_spec((D_MODEL, 2 * D_MODEL)),
            _const_spec((GATE_LORA, RWKV_WIDTH)),
            _const_spec((RWKV_WIDTH, D_MODEL)),
            _const_spec((FOURIER_WIDTH, D_MODEL)),
            _const_spec((D_MODEL, D_MODEL)),
            _const_spec((RWKV_WIDTH, RWKV_WIDTH)),
            _const_spec((LANES, RWKV_WIDTH)),
            _const_spec((1, RWKV_WIDTH)),
            _const_spec((1, RWKV_WIDTH)),
        ],
        out_specs=tok(D_MODEL),
        compiler_params=pltpu.CompilerParams(
            dimension_semantics=("arbitrary", "arbitrary"), vmem_limit_bytes=VMEM_LIMIT),
        name="branch_merge",
    )(x, mod3, yf, yb, bsf, bsb, p_all, p_all, fo, g1, wg, g2, wur, wuf, wo, ehm, ex, lng, lnb)


def _ffn_kernel(x_ref, mod_ref, g2_ref, wgu_ref, wd_ref, gf_ref, o_ref):
    x = x_ref[0]
    shift = mod_ref[0, :, 3 * D_MODEL:4 * D_MODEL]
    scale = mod_ref[0, :, 4 * D_MODEL:5 * D_MODEL]
    gate2 = mod_ref[0, :, 5 * D_MODEL:6 * D_MODEL]
    hx = ((_rms(x) * g2_ref[...]) * (1.0 + scale) + shift).astype(BF16)
    part = D_FF // FFN_SPLIT
    acc = jnp.zeros((FFN_TILE, D_MODEL), F32)
    for s in range(FFN_SPLIT):
        gt = _dot(hx, wgu_ref[:, s * part:(s + 1) * part])
        up = _dot(hx, wgu_ref[:, D_FF + s * part:D_FF + (s + 1) * part])
        h = (gt * _sigmoid(gt) * up).astype(BF16)
        acc = acc + _dot(h, wd_ref[s * part:(s + 1) * part, :])
    o_ref[0] = _rms(x + gate2 * acc) * gf_ref[...]


def _ffn(x1, mod3, g2, wgu, wd, gf):
    tiles = SEQ // FFN_TILE
    return pl.pallas_call(
        _ffn_kernel,
        out_shape=jax.ShapeDtypeStruct((BATCH, SEQ, D_MODEL), F32),
        grid=(BATCH, tiles),
        in_specs=[
            pl.BlockSpec((1, FFN_TILE, D_MODEL), lambda b, t: (b, t, 0)),
            pl.BlockSpec((1, 1, 6 * D_MODEL), lambda b, t: (b, 0, 0)),
            _const_spec((1, D_MODEL)),
            _const_spec((D_MODEL, 2 * D_FF)),
            _const_spec((D_FF, D_MODEL)),
            _const_spec((1, D_MODEL)),
        ],
        out_specs=pl.BlockSpec((1, FFN_TILE, D_MODEL), lambda b, t: (b, t, 0)),
        compiler_params=pltpu.CompilerParams(
            dimension_semantics=("arbitrary", "arbitrary"), vmem_limit_bytes=VMEM_LIMIT),
        name="swiglu_final",
    )(x1, mod3, g2, wgu, wd, gf)


@functools.lru_cache(maxsize=None)
def _constants():
    def dft(n):
        ang = 2.0 * np.pi * (np.outer(np.arange(n), np.arange(n)) % n) / n
        return np.cos(ang), -np.sin(ang)

    a_ch, b_ch = dft(FOURIER_GROUP_DIM)
    a_col, b_col = dft(GRID_W)
    a_row, b_row = dft(GRID_H)
    norm = 1.0 / np.sqrt(GRID_H * GRID_W * FOURIER_GROUP_DIM)
    cd = np.concatenate([a_ch, b_ch], axis=1) * norm
    a_tok = np.kron(a_row, a_col) - np.kron(b_row, b_col)
    b_tok = np.kron(a_row, b_col) + np.kron(b_row, a_col)
    ab = np.concatenate([a_tok, -b_tok], axis=1)[0:HALF_TOK]
    neg = lambda n: (np.arange(n)[:, None] == (-np.arange(n)[None, :]) % n).astype(np.float32)
    jc = neg(GRID_W)
    jj = np.kron(np.eye(FOURIER_GROUPS), neg(FOURIER_GROUP_DIM))
    head = np.arange(RWKV_WIDTH) // HEAD_DIM
    eh = (head[:, None] == head[None, :]).astype(np.float32)
    esel = (head[:, None] == np.arange(LANES)[None, :]).astype(np.float32)
    i = np.arange(TOK_TILE)
    same_chunk = (i[:, None] // CHUNK) == (i[None, :] // CHUNK)
    tri = np.stack([same_chunk & (i[None, :] <= i[:, None]), same_chunk & (i[None, :] >= i[:, None])])
    f32 = lambda u: np.asarray(u, np.float32)
    return dict(cd=f32(cd), ab=f32(ab), jc=f32(jc), jj=f32(jj), eh=f32(eh), ehm=f32(eh / HEAD_DIM), esel=f32(esel), ex=f32(esel.T),
                tri=f32(tri))


def kernel(x, c, ctx, c_ctx, norm1_g, norm2_g, w_ada, b_ada, w_in, mu_prev, mu_next, w0_f, w2_f, a0_f, a2_f, w0_b, w2_b, a0_b, a2_b, g2, k_k, k_a, r_k, lnx_g, lnx_b, w_up_r, w_up_f, w_out, w_gu, w_down, final_norm_g):
    cst = {name: jnp.asarray(val).astype(BF16) for name, val in _constants().items()}
    row = lambda u: u.reshape(1, -1)
    cc = jnp.concatenate(
        [c, c_ctx[None, :], jnp.zeros((MOD_ROWS - BATCH - 1, D_MODEL), F32)], axis=0)
    mod = _modulation(cc, w_ada[0].astype(BF16), row(b_ada[0]))
    mod3 = mod.reshape(MOD_ROWS, 1, 6 * D_MODEL)

    w_in0 = w_in[0].astype(BF16)
    p_all, xf = _inproj(x, ctx, mod3, row(norm1_g[0]), w_in0[:, 0:GATE_START], row(mu_prev[0]), row(mu_next[0]),
                        row(k_k[0]), cst["eh"])

    zeros_lora = jnp.zeros((DECAY_LORA, RWKV_WIDTH), F32)
    wlo = jnp.stack([jnp.concatenate([w2_f[0], zeros_lora], 0), jnp.concatenate([zeros_lora, w2_b[0]], 0)]).astype(BF16)
    alo = jnp.stack([jnp.concatenate([a2_f[0], zeros_lora], 0), jnp.concatenate([zeros_lora, a2_b[0]], 0)]).astype(BF16)
    zero_row = jnp.zeros((RWKV_WIDTH,), F32)
    vecs = jnp.stack([w0_f[0], w0_b[0], a0_f[0], a0_b[0], k_a[0], r_k[0].reshape(-1), zero_row, zero_row])
    yf, yb, bsf, bsb = _wkv(p_all, wlo, alo, vecs, cst["esel"], cst["tri"])

    fo = _fourier(xf, cst["cd"], cst["ab"], cst["jc"], cst["jj"])

    x1 = _merge(x, mod3, yf, yb, bsf, bsb, p_all, fo, row(norm1_g[0]), w_in0[:, GATE_START:],
                g2[0].astype(BF16), w_up_r[0].astype(BF16), w_up_f[0].astype(BF16), w_out[0].astype(BF16),
                cst["ehm"], cst["ex"], row(lnx_g[0]), row(lnx_b[0]))
    return _ffn(x1, mod3, row(norm2_g[0]), w_gu[0].astype(BF16), w_down[0].astype(BF16), row(final_norm_g))
```

```python
import functools

import jax
import jax.numpy as jnp
import numpy as np
from jax import lax
from jax.experimental import pallas as pl
from jax.experimental.pallas import tpu as pltpu

D_MODEL = 1024
BATCH = 16
SEQ = 2048
GRID_W = 64
CTX_LEN = 256
FOURIER_WIDTH = 512
FOURIER_GROUPS = 4
FOURIER_GROUP_DIM = FOURIER_WIDTH // FOURIER_GROUPS
RWKV_WIDTH = 512
HEAD_DIM = 64
HEADS = RWKV_WIDTH // HEAD_DIM
DECAY_LORA = 64
AAA_LORA = 64
GATE_LORA = 128
RWKV_COLS = 3 * RWKV_WIDTH + 2 * DECAY_LORA + 2 * AAA_LORA + GATE_LORA
FOURIER_START = RWKV_COLS
GATE_START = RWKV_COLS + FOURIER_WIDTH
D_FF = 2816
NORM_EPS = 1e-6
GN_EPS = 64e-5
DECAY_SCALE = float(np.exp(-0.5))

COL_R, COL_K, COL_V = 0, RWKV_WIDTH, 2 * RWKV_WIDTH
COL_WD = 3 * RWKV_WIDTH
COL_AD = COL_WD + 2 * DECAY_LORA
COL_GD = COL_AD + 2 * AAA_LORA
COL_KK = RWKV_COLS
SLAB_COLS = RWKV_COLS + RWKV_WIDTH

LANES = 128
SUBLANES = 8
TOK_TILE = 256
MM_TILE = 512
CHUNK = 64
SEQ_ALL = SEQ + CTX_LEN
N_TILES = SEQ_ALL // TOK_TILE
N_LAT_TILES = SEQ // TOK_TILE
FFN_TILE = 512
FFN_SPLIT = 11
MOD_ROWS = 24
VMEM_LIMIT = 56 * 1024 * 1024

F32 = jnp.float32
BF16 = jnp.bfloat16


def _dot(a, b):
    return jnp.dot(a, b, preferred_element_type=F32)


def _dot_nt(a, b):
    return lax.dot_general(a, b, (((1,), (1,)), ((), ())), preferred_element_type=F32)


def _dot_tn(a, b):
    return lax.dot_general(a, b, (((0,), (0,)), ((), ())), preferred_element_type=F32)


def _dot_hilo(x, e):
    hi = x.astype(BF16)
    lo = (x - hi.astype(F32)).astype(BF16)
    return _dot(hi, e) + _dot(lo, e)


def _dot_split3(e, x):
    x1 = x.astype(BF16)
    rem = x - x1.astype(F32)
    x2 = rem.astype(BF16)
    x3 = (rem - x2.astype(F32)).astype(BF16)
    return _dot(e, x1) + _dot(e, x2) + _dot(e, x3)


def _rms(u):
    return u * lax.rsqrt(jnp.mean(u * u, axis=-1, keepdims=True) + NORM_EPS)


def _sigmoid(z):
    return 1.0 / (1.0 + jnp.exp(-z))


def _softplus(z):
    return jnp.maximum(z, 0.0) + jnp.log(1.0 + jnp.exp(-jnp.abs(z)))


def _const_spec(shape):
    nd = len(shape)
    return pl.BlockSpec(shape, lambda *_: (0,) * nd, pipeline_mode=pl.Buffered(1))


def _mod_kernel(c_ref, w_ref, b_ref, o_ref):
    c = c_ref[...]
    s = (c * _sigmoid(c)).astype(BF16)
    o_ref[...] = _dot(s, w_ref[...]) + b_ref[...]


def _modulation(cc, w_ada, b_ada):
    n_blk = 4
    bn = (6 * D_MODEL) // n_blk
    return pl.pallas_call(
        _mod_kernel,
        out_shape=jax.ShapeDtypeStruct((MOD_ROWS, 6 * D_MODEL), F32),
        grid=(n_blk,),
        in_specs=[
            pl.BlockSpec((MOD_ROWS, D_MODEL), lambda n: (0, 0)),
            pl.BlockSpec((D_MODEL, bn), lambda n: (0, n)),
            pl.BlockSpec((1, bn), lambda n: (0, n)),
        ],
        out_specs=pl.BlockSpec((MOD_ROWS, bn), lambda n: (0, n)),
        name="adaln_mod",
    )(cc, w_ada, b_ada)


def _norm_mod(u, g1_ref, mod_ref):
    shift = mod_ref[0, :, 0:D_MODEL]
    scale = mod_ref[0, :, D_MODEL:2 * D_MODEL]
    return (_rms(u) * g1_ref[...]) * (1.0 + scale) + shift


N_LAT_STEPS = SEQ // MM_TILE


def _inproj_kernel(x_ref, xp_ref, xn_ref, c_ref, modb_ref, modc_ref, g1_ref, w_ref, mup_ref, mun_ref, kk_ref, eh_ref,
                   p_ref, xf_ref):
    j = pl.program_id(1)

    def normalised_keys(rows):
        kq = p_ref[0, 0:rows, COL_K:COL_K + RWKV_WIDTH] * kk_ref[...]
        ss = _dot((kq * kq).astype(BF16), eh_ref[...])
        p_ref[0, 0:rows, COL_KK:COL_KK + RWKV_WIDTH] = kq * lax.rsqrt(jnp.maximum(ss, 1e-24))

    @pl.when(j < N_LAT_STEPS)
    def _():
        prev_ok = jnp.where(j >= 1, 1.0, 0.0)
        next_ok = jnp.where(j <= N_LAT_STEPS - 2, 1.0, 0.0)
        h_ext = jnp.concatenate(
            [_norm_mod(xp_ref[0], g1_ref, modb_ref) * prev_ok, _norm_mod(x_ref[0], g1_ref, modb_ref),
             _norm_mod(xn_ref[0], g1_ref, modb_ref) * next_ok], axis=0).astype(BF16)
        xf_ref[0] = _dot(h_ext[SUBLANES:SUBLANES + MM_TILE], w_ref[:, FOURIER_START:GATE_START]).astype(BF16)
        n_ext = MM_TILE + 2 * SUBLANES
        p_ext = _dot(h_ext, w_ref[:, 0:RWKV_COLS])
        pm = p_ext[SUBLANES:SUBLANES + MM_TILE]
        pu = pltpu.roll(p_ext, 1, 0)[SUBLANES:SUBLANES + MM_TILE]
        pd = pltpu.roll(p_ext, n_ext - 1, 0)[SUBLANES:SUBLANES + MM_TILE]
        p_ref[0, :, 0:RWKV_COLS] = pm + mup_ref[...] * (pu - pm) + mun_ref[...] * (pd - pm)
        normalised_keys(MM_TILE)

    @pl.when(j == N_LAT_STEPS)
    def _():
        hc = _norm_mod(c_ref[0], g1_ref, modc_ref).astype(BF16)
        pm = _dot(hc, w_ref[:, 0:RWKV_COLS])
        t = lax.broadcasted_iota(jnp.int32, (CTX_LEN, RWKV_COLS), 0)
        pu = jnp.where(t == 0, 0.0, pltpu.roll(pm, 1, 0))
        pd = jnp.where(t == CTX_LEN - 1, 0.0, pltpu.roll(pm, CTX_LEN - 1, 0))
        p_ref[0, 0:CTX_LEN, 0:RWKV_COLS] = pm + mup_ref[...] * (pu - pm) + mun_ref[...] * (pd - pm)
        normalised_keys(CTX_LEN)


def _inproj(x, ctx, mod3, g1, w_rf, mu_prev, mu_next, k_k, eh):
    halo_blocks = MM_TILE // SUBLANES
    last_halo = SEQ // SUBLANES - 1
    lat = lambda j: jnp.minimum(j, N_LAT_STEPS - 1)
    return pl.pallas_call(
        _inproj_kernel,
        out_shape=(jax.ShapeDtypeStruct((BATCH, SEQ_ALL, SLAB_COLS), F32),
                   jax.ShapeDtypeStruct((BATCH, SEQ, FOURIER_WIDTH), BF16)),
        grid=(BATCH, N_LAT_STEPS + 1),
        in_specs=[
            pl.BlockSpec((1, MM_TILE, D_MODEL), lambda b, j: (b, lat(j), 0)),
            pl.BlockSpec((1, SUBLANES, D_MODEL), lambda b, j: (b, jnp.maximum(lat(j) * halo_blocks - 1, 0), 0)),
            pl.BlockSpec((1, SUBLANES, D_MODEL),
                         lambda b, j: (b, jnp.minimum((lat(j) + 1) * halo_blocks, last_halo), 0)),
            pl.BlockSpec((1, CTX_LEN, D_MODEL), lambda b, j: (b, 0, 0)),
            pl.BlockSpec((1, 1, 2 * D_MODEL), lambda b, j: (b, 0, 0)),
            pl.BlockSpec((1, 1, 2 * D_MODEL), lambda b, j: (BATCH, 0, 0)),
            _const_spec((1, D_MODEL)),
            _const_spec((D_MODEL, GATE_START)),
            _const_spec((1, RWKV_COLS)),
            _const_spec((1, RWKV_COLS)),
            _const_spec((1, RWKV_WIDTH)),
            _const_spec((RWKV_WIDTH, RWKV_WIDTH)),
        ],
        out_specs=(
            pl.BlockSpec((1, MM_TILE, SLAB_COLS), lambda b, j: (b, j, 0)),
            pl.BlockSpec((1, MM_TILE, FOURIER_WIDTH), lambda b, j: (b, lat(j), 0)),
        ),
        compiler_params=pltpu.CompilerParams(
            dimension_semantics=("arbitrary", "arbitrary"), vmem_limit_bytes=VMEM_LIMIT),
        name="inproj_shift",
    )(x, x, x, ctx, mod3, mod3, g1, w_rf, mu_prev, mu_next, k_k, eh)


_S_KD, _S_BE, _S_LD, _S_LGI = range(4)
_N_PLANES = 4
_V_W0F, _V_W0B, _V_A0F, _V_A0B, _V_KA, _V_RK = range(6)


def _wkv_kernel(pf_ref, pb_ref, wlo_ref, alo_ref, vec_ref, esel_ref, tri_ref,
                yf_ref, yb_ref, bsf_ref, bsb_ref, zt_ref, s_ref):
    j = pl.program_id(1)

    @pl.when(j == 0)
    def _():
        zt_ref[...] = jnp.zeros_like(zt_ref)

    k_a = vec_ref[_V_KA:_V_KA + 1, :]
    r_k = vec_ref[_V_RK:_V_RK + 1, :]
    p_refs = (pf_ref, pb_ref)

    for d, bs_ref in enumerate((bsf_ref, bsb_ref)):
        p_ref = p_refs[d]
        w0 = vec_ref[_V_W0F + d:_V_W0F + d + 1, :]
        a0 = vec_ref[_V_A0F + d:_V_A0F + d + 1, :]
        k = p_ref[0, :, COL_K:COL_K + RWKV_WIDTH]
        wa = p_ref[0, :, COL_WD:COL_WD + LANES]
        aa = p_ref[0, :, COL_AD:COL_AD + LANES]
        wl = w0 + _dot(jnp.tanh(wa).astype(BF16), wlo_ref[d])
        ld = -DECAY_SCALE * _sigmoid(wl)
        a = _sigmoid(a0 + _dot(aa.astype(BF16), alo_ref[d]))
        kd = k * (1.0 + (a - 1.0) * k_a)
        base = d * _N_PLANES
        s_ref[base + _S_KD] = kd
        s_ref[base + _S_BE] = p_ref[0, :, COL_KK:COL_KK + RWKV_WIDTH] * a
        s_ref[base + _S_LD] = ld
        s_ref[base + _S_LGI] = _dot_split3(tri_ref[d], ld)
        bs_ref[0] = _dot((p_ref[0, :, COL_R:COL_R + RWKV_WIDTH] * kd * r_k).astype(BF16), esel_ref[...])

    ri = lax.broadcasted_iota(jnp.int32, (CHUNK, LANES), 0)
    li = lax.broadcasted_iota(jnp.int32, (CHUNK, LANES), 1)
    ci = jnp.bitwise_and(li, HEAD_DIM - 1)
    lane_lo = li < HEAD_DIM
    eye = ri == ci
    strict = (ci < ri, ci > ri)
    incl = (ci <= ri, ci >= ri)
    merge_masks = [jnp.right_shift(ri, 1) == jnp.right_shift(ci, 1)]
    for sh in range(1, CHUNK.bit_length() - 1):
        merge_masks.append(jnp.logical_and(jnp.right_shift(ri, sh + 1) == jnp.right_shift(ci, sh + 1),
                                           jnp.right_shift(ri, sh) != jnp.right_shift(ci, sh)))
    n2 = 2 * CHUNK
    same_head = (lax.broadcasted_iota(jnp.int32, (n2, LANES), 0) >= CHUNK) == \
        (lax.broadcasted_iota(jnp.int32, (n2, LANES), 1) >= HEAD_DIM)

    def stack(u):
        return jnp.concatenate([jnp.where(lane_lo, u, 0.0), jnp.where(lane_lo, 0.0, u)], axis=0)

    def load(d, p, r0):
        base = d * _N_PLANES
        rows = slice(r0, r0 + CHUNK)
        plane = lambda pln: s_ref[base + pln, rows, p * LANES:(p + 1) * LANES]
        col = lambda c0: p_refs[d][0, rows, c0 + p * LANES:c0 + (p + 1) * LANES]
        return (plane(_S_LD), plane(_S_LGI), col(COL_R), col(COL_V), plane(_S_KD), -col(COL_KK), plane(_S_BE))

    def phase_a(dirs, ins):
        us = range(len(dirs))
        ld, lgi, r, v, kd, al, be = ([ins[u][f] for u in us] for f in range(7))
        m = [lgi[u][CHUNK // 2:CHUNK // 2 + 1, :] for u in us]
        last = [CHUNK - 1 if dirs[u] == 0 else 0 for u in us]
        lgc = [lgi[u][last[u]:last[u] + 1, :] for u in us]
        lge = [lgi[u] - ld[u] for u in us]
        e_m = [jnp.exp(m[u] - lgi[u]) for u in us]
        lhs = [jnp.concatenate([al[u] * jnp.exp(lge[u] - m[u]), r[u] * jnp.exp(lgi[u] - m[u])], axis=0).astype(BF16)
               for u in us]
        rhs = [jnp.concatenate([stack(be[u] * e_m[u]), stack(kd[u] * e_m[u])], axis=0).astype(BF16) for u in us]
        sc = [_dot_nt(lhs[u], rhs[u]) for u in us]
        l_ab = [jnp.where(strict[dirs[u]], sc[u][0:CHUNK, 0:LANES], 0.0) for u in us]

        t_m = [jnp.where(eye, 1.0, jnp.where(merge_masks[0], l_ab[u], 0.0)) for u in us]
        for mk in merge_masks[1:]:
            t_bd = [stack(t_m[u]).astype(BF16) for u in us]
            e_l = [_dot(jnp.where(mk, l_ab[u], 0.0).astype(BF16), t_bd[u]) for u in us]
            t_m = [t_m[u] + _dot(t_m[u].astype(BF16), stack(e_l[u]).astype(BF16)) for u in us]
        t_b = [t_m[u].astype(BF16) for u in us]

        v2 = [stack(v[u]).astype(BF16) for u in us]
        nv = [_dot(jnp.concatenate([jnp.where(strict[dirs[u]], sc[u][0:CHUNK, LANES:2 * LANES], 0.0),
                                    jnp.where(incl[dirs[u]], sc[u][CHUNK:n2, LANES:2 * LANES], 0.0)],
                                   axis=0).astype(BF16), v2[u]) for u in us]
        wu = [_dot(t_b[u], jnp.concatenate([stack(al[u] * jnp.exp(lge[u])), stack(nv[u][0:CHUNK])],
                                           axis=1).astype(BF16)) for u in us]
        m_rb = [jnp.where(incl[dirs[u]], sc[u][CHUNK:n2, 0:LANES], 0.0).astype(BF16) for u in us]
        qy = [_dot(m_rb[u], jnp.concatenate([stack(wu[u][:, 0:LANES]), stack(wu[u][:, LANES:2 * LANES])],
                                            axis=1).astype(BF16)) for u in us]
        q = [(r[u] * jnp.exp(lgi[u]) + qy[u][:, 0:LANES]).astype(BF16) for u in us]
        y0 = [nv[u][CHUNK:n2] + qy[u][:, LANES:2 * LANES] for u in us]
        e_c = [jnp.exp(lgc[u] - lgi[u]) for u in us]
        b_e = [(be[u] * e_c[u]).astype(BF16) for u in us]
        g_l = [jnp.where(same_head, _dot_tn(wu[u][:, 0:LANES].astype(BF16), b_e[u]), 0.0).astype(BF16)
               for u in us]
        h_f = [_dot_tn(jnp.concatenate([wu[u][:, LANES:2 * LANES], v[u]], axis=0).astype(BF16),
                       jnp.concatenate([b_e[u], (kd[u] * e_c[u]).astype(BF16)], axis=0)) for u in us]
        h_t = [jnp.where(lane_lo, h_f[u][0:CHUNK], h_f[u][CHUNK:n2]) for u in us]
        g_c = [jnp.exp(lgc[u]) for u in us]
        return q, y0, g_l, h_t, g_c

    n_chunks = TOK_TILE // CHUNK
    pairs = HEADS // 2
    units = [(d, p, i if d == 0 else n_chunks - 1 - i)
             for i in range(n_chunks) for d in range(2) for p in range(pairs)]
    ins = [load(d, p, c * CHUNK) for d, p, c in units]
    q, y0, g_l, h_t, g_c = phase_a([d for d, _, _ in units], ins)
    y_refs = (yf_ref, yb_ref)
    z = {(d, p): zt_ref[d, p] for d in range(2) for p in range(pairs)}
    for u, (d, p, c) in enumerate(units):
        y_refs[d][0, c * CHUNK:(c + 1) * CHUNK, p * LANES:(p + 1) * LANES] = \
            _dot_nt(q[u], stack(z[d, p]).astype(BF16)) + y0[u]
        z[d, p] = z[d, p] * g_c[u] + _dot(z[d, p].astype(BF16), g_l[u]) + h_t[u]
    for (d, p), val in z.items():
        zt_ref[d, p] = val


def _wkv(p_all, wlo, alo, vecs, esel, tri):
    def fwd_tile(j):
        return jnp.where(j == 0, N_LAT_TILES, j - 1)

    def bwd_tile(j):
        return jnp.where(j == 0, N_LAT_TILES, N_LAT_TILES - j)

    def bwd_out(j):
        return jnp.where(j == 0, N_LAT_TILES - 1, N_LAT_TILES - j)

    y_shape = jax.ShapeDtypeStruct((BATCH, SEQ, RWKV_WIDTH), F32)
    bs_shape = jax.ShapeDtypeStruct((BATCH, SEQ, LANES), F32)
    return pl.pallas_call(
        _wkv_kernel,
        out_shape=(y_shape, y_shape, bs_shape, bs_shape),
        grid=(BATCH, N_TILES),
        in_specs=[
            pl.BlockSpec((1, TOK_TILE, SLAB_COLS), lambda b, j: (b, fwd_tile(j), 0)),
            pl.BlockSpec((1, TOK_TILE, SLAB_COLS), lambda b, j: (b, bwd_tile(j), 0)),
            _const_spec((2, LANES, RWKV_WIDTH)),
            _const_spec((2, LANES, RWKV_WIDTH)),
            _const_spec((SUBLANES, RWKV_WIDTH)),
            _const_spec((RWKV_WIDTH, LANES)),
            _const_spec((2, TOK_TILE, TOK_TILE)),
        ],
        out_specs=(
            pl.BlockSpec((1, TOK_TILE, RWKV_WIDTH), lambda b, j: (b, jnp.maximum(j - 1, 0), 0)),
            pl.BlockSpec((1, TOK_TILE, RWKV_WIDTH), lambda b, j: (b, bwd_out(j), 0)),
            pl.BlockSpec((1, TOK_TILE, LANES), lambda b, j: (b, jnp.maximum(j - 1, 0), 0)),
            pl.BlockSpec((1, TOK_TILE, LANES), lambda b, j: (b, bwd_out(j), 0)),
        ),
        scratch_shapes=[
            pltpu.VMEM((2, HEADS // 2, HEAD_DIM, LANES), F32),
            pltpu.VMEM((2 * _N_PLANES, TOK_TILE, RWKV_WIDTH), F32),
        ],
        compiler_params=pltpu.CompilerParams(
            dimension_semantics=("arbitrary", "arbitrary"), vmem_limit_bytes=VMEM_LIMIT),
        name="wkv7_chunked",
    )(p_all, p_all, wlo, alo, vecs, esel, tri)


GRID_H = SEQ // GRID_W
HALF_ROWS = GRID_H // 2 + 1
HALF_TOK = HALF_ROWS * GRID_W
MIRROR_TOK = SEQ - HALF_TOK


def _fourier_kernel(x_ref, cd_ref, ab_ref, jc_ref, jj_ref, o_ref, rhs_ref, m_ref):
    for g in range(FOURIER_GROUPS):
        gs = slice(g * FOURIER_GROUP_DIM, (g + 1) * FOURIER_GROUP_DIM)
        z = _dot(x_ref[0, :, gs], cd_ref[...])
        rhs_ref[0:SEQ, gs] = z[:, 0:FOURIER_GROUP_DIM].astype(BF16)
        rhs_ref[SEQ:2 * SEQ, gs] = z[:, FOURIER_GROUP_DIM:2 * FOURIER_GROUP_DIM].astype(BF16)
    yh = _dot(ab_ref[...], rhs_ref[...]).astype(BF16)
    o_ref[0, 0:HALF_TOK, :] = yh
    for k in range(GRID_H - HALF_ROWS):
        src = GRID_H - HALF_ROWS - k
        m_ref[k * GRID_W:(k + 1) * GRID_W, :] = _dot(
            jc_ref[...], yh[src * GRID_W:(src + 1) * GRID_W, :]).astype(BF16)
    o_ref[0, HALF_TOK:SEQ, :] = _dot(m_ref[...], jj_ref[...]).astype(BF16)


def _fourier(xf, cd, ab, jc, jj):
    return pl.pallas_call(
        _fourier_kernel,
        out_shape=jax.ShapeDtypeStruct((BATCH, SEQ, FOURIER_WIDTH), BF16),
        grid=(BATCH,),
        in_specs=[
            pl.BlockSpec((1, SEQ, FOURIER_WIDTH), lambda b: (b, 0, 0)),
            _const_spec((FOURIER_GROUP_DIM, 2 * FOURIER_GROUP_DIM)),
            _const_spec((HALF_TOK, 2 * SEQ)),
            _const_spec((GRID_W, GRID_W)),
            _const_spec((FOURIER_WIDTH, FOURIER_WIDTH)),
        ],
        out_specs=pl.BlockSpec((1, SEQ, FOURIER_WIDTH), lambda b: (b, 0, 0)),
        scratch_shapes=[pltpu.VMEM((2 * SEQ, FOURIER_WIDTH), BF16),
                        pltpu.VMEM((MIRROR_TOK, FOURIER_WIDTH), BF16)],
        compiler_params=pltpu.CompilerParams(vmem_limit_bytes=VMEM_LIMIT),
        name="fourier_dft",
    )(xf, cd, ab, jc, jj)


def _merge_kernel(x_ref, mod_ref, yf_ref, yb_ref, bsf_ref, bsb_ref, v_ref, gd_ref, fo_ref,
                  g1_ref, wg_ref, g2_ref, wur_ref, wuf_ref, wo_ref, ehm_ref, ex_ref, lng_ref, lnb_ref, o_ref):
    x = x_ref[0]
    shift = mod_ref[0, :, 0:D_MODEL]
    scale = mod_ref[0, :, D_MODEL:2 * D_MODEL]
    gate1 = mod_ref[0, :, 2 * D_MODEL:3 * D_MODEL]
    hx = ((_rms(x) * g1_ref[...]) * (1.0 + scale) + shift).astype(BF16)
    gates = _dot(hx, wg_ref[...])

    y = yf_ref[0] + yb_ref[0]
    mean = _dot(y.astype(BF16), ehm_ref[...])
    dy = y - mean
    var = _dot((dy * dy).astype(BF16), ehm_ref[...])
    o = dy * lax.rsqrt(var + GN_EPS) * lng_ref[...] + lnb_ref[...]
    bonus = _dot((bsf_ref[0] + bsb_ref[0]).astype(BF16), ex_ref[...]) * v_ref[0]
    g = _dot(_sigmoid(gd_ref[0]).astype(BF16), g2_ref[...])
    o = ((o + bonus) * g).astype(BF16)
    r_up = _dot(o, wur_ref[...])
    f_up = _dot(fo_ref[0], wuf_ref[...])
    mix = (_sigmoid(gates[:, 0:D_MODEL]) * f_up + _sigmoid(gates[:, D_MODEL:2 * D_MODEL]) * r_up).astype(BF16)
    o_ref[0] = x + gate1 * _dot(mix, wo_ref[...])


def _merge(x, mod3, yf, yb, bsf, bsb, p_all, fo, g1, wg, g2, wur, wuf, wo, ehm, ex, lng, lnb):
    tok = lambda w: pl.BlockSpec((1, MM_TILE, w), lambda b, t: (b, t, 0))
    return pl.pallas_call(
        _merge_kernel,
        out_shape=jax.ShapeDtypeStruct((BATCH, SEQ, D_MODEL), F32),
        grid=(BATCH, SEQ // MM_TILE),
        in_specs=[
            tok(D_MODEL),
            pl.BlockSpec((1, 1, 6 * D_MODEL), lambda b, t: (b, 0, 0)),
            tok(RWKV_WIDTH), tok(RWKV_WIDTH), tok(LANES), tok(LANES),
            pl.BlockSpec((1, MM_TILE, RWKV_WIDTH), lambda b, t: (b, t, COL_V // RWKV_WIDTH)),
            pl.BlockSpec((1, MM_TILE, GATE_LORA), lambda b, t: (b, t, COL_GD // GATE_LORA)),
            tok(FOURIER_WIDTH),
            _const_spec((1, D_MODEL)),
            _const_spec((D_MODEL, 2 * D_MODEL)),
            _const_spec((GATE_LORA, RWKV_WIDTH)),
            _const_spec((RWKV_WIDTH, D_MODEL)),
            _const_spec((FOURIER_WIDTH, D_MODEL)),
            _const_spec((D_MODEL, D_MODEL)),
            _const_spec((RWKV_WIDTH, RWKV_WIDTH)),
            _const_spec((LANES, RWKV_WIDTH)),
            _const_spec((1, RWKV_WIDTH)),
            _const_spec((1, RWKV_WIDTH)),
        ],
        out_specs=tok(D_MODEL),
        compiler_params=pltpu.CompilerParams(
            dimension_semantics=("arbitrary", "arbitrary"), vmem_limit_bytes=VMEM_LIMIT),
        name="branch_merge",
    )(x, mod3, yf, yb, bsf, bsb, p_all, p_all, fo, g1, wg, g2, wur, wuf, wo, ehm, ex, lng, lnb)


def _ffn_kernel(x_ref, mod_ref, g2_ref, wgu_ref, wd_ref, gf_ref, o_ref):
    x = x_ref[0]
    shift = mod_ref[0, :, 3 * D_MODEL:4 * D_MODEL]
    scale = mod_ref[0, :, 4 * D_MODEL:5 * D_MODEL]
    gate2 = mod_ref[0, :, 5 * D_MODEL:6 * D_MODEL]
    hx = ((_rms(x) * g2_ref[...]) * (1.0 + scale) + shift).astype(BF16)
    part = D_FF // FFN_SPLIT
    acc = jnp.zeros((FFN_TILE, D_MODEL), F32)
    for s in range(FFN_SPLIT):
        gt = _dot(hx, wgu_ref[:, s * part:(s + 1) * part])
        up = _dot(hx, wgu_ref[:, D_FF + s * part:D_FF + (s + 1) * part])
        h = (gt * _sigmoid(gt) * up).astype(BF16)
        acc = acc + _dot(h, wd_ref[s * part:(s + 1) * part, :])
    o_ref[0] = _rms(x + gate2 * acc) * gf_ref[...]


def _ffn(x1, mod3, g2, wgu, wd, gf):
    tiles = SEQ // FFN_TILE
    return pl.pallas_call(
        _ffn_kernel,
        out_shape=jax.ShapeDtypeStruct((BATCH, SEQ, D_MODEL), F32),
        grid=(BATCH, tiles),
        in_specs=[
            pl.BlockSpec((1, FFN_TILE, D_MODEL), lambda b, t: (b, t, 0)),
            pl.BlockSpec((1, 1, 6 * D_MODEL), lambda b, t: (b, 0, 0)),
            _const_spec((1, D_MODEL)),
            _const_spec((D_MODEL, 2 * D_FF)),
            _const_spec((D_FF, D_MODEL)),
            _const_spec((1, D_MODEL)),
        ],
        out_specs=pl.BlockSpec((1, FFN_TILE, D_MODEL), lambda b, t: (b, t, 0)),
        compiler_params=pltpu.CompilerParams(
            dimension_semantics=("arbitrary", "arbitrary"), vmem_limit_bytes=VMEM_LIMIT),
        name="swiglu_final",
    )(x1, mod3, g2, wgu, wd, gf)


@functools.lru_cache(maxsize=None)
def _constants():
    def dft(n):
        ang = 2.0 * np.pi * (np.outer(np.arange(n), np.arange(n)) % n) / n
        return np.cos(ang), -np.sin(ang)

    a_ch, b_ch = dft(FOURIER_GROUP_DIM)
    a_col, b_col = dft(GRID_W)
    a_row, b_row = dft(GRID_H)
    norm = 1.0 / np.sqrt(GRID_H * GRID_W * FOURIER_GROUP_DIM)
    cd = np.concatenate([a_ch, b_ch], axis=1) * norm
    a_tok = np.kron(a_row, a_col) - np.kron(b_row, b_col)
    b_tok = np.kron(a_row, b_col) + np.kron(b_row, a_col)
    ab = np.concatenate([a_tok, -b_tok], axis=1)[0:HALF_TOK]
    neg = lambda n: (np.arange(n)[:, None] == (-np.arange(n)[None, :]) % n).astype(np.float32)
    jc = neg(GRID_W)
    jj = np.kron(np.eye(FOURIER_GROUPS), neg(FOURIER_GROUP_DIM))
    head = np.arange(RWKV_WIDTH) // HEAD_DIM
    eh = (head[:, None] == head[None, :]).astype(np.float32)
    esel = (head[:, None] == np.arange(LANES)[None, :]).astype(np.float32)
    i = np.arange(TOK_TILE)
    same_chunk = (i[:, None] // CHUNK) == (i[None, :] // CHUNK)
    tri = np.stack([same_chunk & (i[None, :] <= i[:, None]), same_chunk & (i[None, :] >= i[:, None])])
    f32 = lambda u: np.asarray(u, np.float32)
    return dict(cd=f32(cd), ab=f32(ab), jc=f32(jc), jj=f32(jj), eh=f32(eh), ehm=f32(eh / HEAD_DIM), esel=f32(esel), ex=f32(esel.T),
                tri=f32(tri))


def kernel(x, c, ctx, c_ctx, norm1_g, norm2_g, w_ada, b_ada, w_in, mu_prev, mu_next, w0_f, w2_f, a0_f, a2_f, w0_b, w2_b, a0_b, a2_b, g2, k_k, k_a, r_k, lnx_g, lnx_b, w_up_r, w_up_f, w_out, w_gu, w_down, final_norm_g):
    cst = {name: jnp.asarray(val).astype(BF16) for name, val in _constants().items()}
    row = lambda u: u.reshape(1, -1)
    cc = jnp.concatenate(
        [c, c_ctx[None, :], jnp.zeros((MOD_ROWS - BATCH - 1, D_MODEL), F32)], axis=0)
    mod = _modulation(cc, w_ada[0].astype(BF16), row(b_ada[0]))
    mod3 = mod.reshape(MOD_ROWS, 1, 6 * D_MODEL)

    w_in0 = w_in[0].astype(BF16)
    p_all, xf = _inproj(x, ctx, mod3, row(norm1_g[0]), w_in0[:, 0:GATE_START], row(mu_prev[0]), row(mu_next[0]),
                        row(k_k[0]), cst["eh"])

    zeros_lora = jnp.zeros((DECAY_LORA, RWKV_WIDTH), F32)
    wlo = jnp.stack([jnp.concatenate([w2_f[0], zeros_lora], 0), jnp.concatenate([zeros_lora, w2_b[0]], 0)]).astype(BF16)
    alo = jnp.stack([jnp.concatenate([a2_f[0], zeros_lora], 0), jnp.concatenate([zeros_lora, a2_b[0]], 0)]).astype(BF16)
    zero_row = jnp.zeros((RWKV_WIDTH,), F32)
    vecs = jnp.stack([w0_f[0], w0_b[0], a0_f[0], a0_b[0], k_a[0], r_k[0].reshape(-1), zero_row, zero_row])
    yf, yb, bsf, bsb = _wkv(p_all, wlo, alo, vecs, cst["esel"], cst["tri"])

    fo = _fourier(xf, cst["cd"], cst["ab"], cst["jc"], cst["jj"])

    x1 = _merge(x, mod3, yf, yb, bsf, bsb, p_all, fo, row(norm1_g[0]), w_in0[:, GATE_START:],
                g2[0].astype(BF16), w_up_r[0].astype(BF16), w_up_f[0].astype(BF16), w_out[0].astype(BF16),
                cst["ehm"], cst["ex"], row(lnx_g[0]), row(lnx_b[0]))
    return _ffn(x1, mod3, row(norm2_g[0]), w_gu[0].astype(BF16), w_down[0].astype(BF16), row(final_norm_g))
```

```python
import functools

import jax
import jax.numpy as jnp
import numpy as np
from jax import lax
from jax.experimental import pallas as pl
from jax.experimental.pallas import tpu as pltpu

D_MODEL = 1024
BATCH = 16
SEQ = 2048
GRID_W = 64
CTX_LEN = 256
FOURIER_WIDTH = 512
FOURIER_GROUPS = 4
FOURIER_GROUP_DIM = FOURIER_WIDTH // FOURIER_GROUPS
RWKV_WIDTH = 512
HEAD_DIM = 64
HEADS = RWKV_WIDTH // HEAD_DIM
DECAY_LORA = 64
AAA_LORA = 64
GATE_LORA = 128
RWKV_COLS = 3 * RWKV_WIDTH + 2 * DECAY_LORA + 2 * AAA_LORA + GATE_LORA
FOURIER_START = RWKV_COLS
GATE_START = RWKV_COLS + FOURIER_WIDTH
D_FF = 2816
NORM_EPS = 1e-6
GN_EPS = 64e-5
DECAY_SCALE = float(np.exp(-0.5))

COL_R, COL_K, COL_V = 0, RWKV_WIDTH, 2 * RWKV_WIDTH
COL_WD = 3 * RWKV_WIDTH
COL_AD = COL_WD + 2 * DECAY_LORA
COL_GD = COL_AD + 2 * AAA_LORA
COL_KK = RWKV_COLS
SLAB_COLS = RWKV_COLS + RWKV_WIDTH

LANES = 128
SUBLANES = 8
TOK_TILE = 256
MM_TILE = 512
CHUNK = 64
SEQ_ALL = SEQ + CTX_LEN
N_TILES = SEQ_ALL // TOK_TILE
N_LAT_TILES = SEQ // TOK_TILE
FFN_TILE = 512
FFN_SPLIT = 11
MOD_ROWS = 24
VMEM_LIMIT = 56 * 1024 * 1024

F32 = jnp.float32
BF16 = jnp.bfloat16


def _dot(a, b):
    return jnp.dot(a, b, preferred_element_type=F32)


def _dot_nt(a, b):
    return lax.dot_general(a, b, (((1,), (1,)), ((), ())), preferred_element_type=F32)


def _dot_tn(a, b):
    return lax.dot_general(a, b, (((0,), (0,)), ((), ())), preferred_element_type=F32)


def _dot_split3(e, x):
    x1 = x.astype(BF16)
    rem = x - x1.astype(F32)
    x2 = rem.astype(BF16)
    x3 = (rem - x2.astype(F32)).astype(BF16)
    return _dot(e, x1) + _dot(e, x2) + _dot(e, x3)


def _rms(u):
    return u * lax.rsqrt(jnp.mean(u * u, axis=-1, keepdims=True) + NORM_EPS)


def _sigmoid(z):
    return 1.0 / (1.0 + jnp.exp(-z))


def _const_spec(shape):
    nd = len(shape)
    return pl.BlockSpec(shape, lambda *_: (0,) * nd, pipeline_mode=pl.Buffered(1))


def _mod_kernel(c_ref, w_ref, b_ref, o_ref):
    c = c_ref[...]
    s = (c * _sigmoid(c)).astype(BF16)
    o_ref[...] = _dot(s, w_ref[...]) + b_ref[...]


def _modulation(cc, w_ada, b_ada):
    n_blk = 4
    bn = (6 * D_MODEL) // n_blk
    return pl.pallas_call(
        _mod_kernel,
        out_shape=jax.ShapeDtypeStruct((MOD_ROWS, 6 * D_MODEL), F32),
        grid=(n_blk,),
        in_specs=[
            pl.BlockSpec((MOD_ROWS, D_MODEL), lambda n: (0, 0)),
            pl.BlockSpec((D_MODEL, bn), lambda n: (0, n)),
            pl.BlockSpec((1, bn), lambda n: (0, n)),
        ],
        out_specs=pl.BlockSpec((MOD_ROWS, bn), lambda n: (0, n)),
        name="adaln_mod",
    )(cc, w_ada, b_ada)


def _norm_mod(u, g1_ref, mod_ref):
    shift = mod_ref[0, :, 0:D_MODEL]
    scale = mod_ref[0, :, D_MODEL:2 * D_MODEL]
    return (_rms(u) * g1_ref[...]) * (1.0 + scale) + shift


N_LAT_STEPS = SEQ // MM_TILE


def _inproj_kernel(x_ref, xp_ref, xn_ref, c_ref, modb_ref, modc_ref, g1_ref, w_ref, mup_ref, mun_ref, kk_ref, eh_ref,
                   p_ref, xf_ref):
    j = pl.program_id(1)

    def normalised_keys(rows):
        kq = p_ref[0, 0:rows, COL_K:COL_K + RWKV_WIDTH] * kk_ref[...]
        ss = _dot((kq * kq).astype(BF16), eh_ref[...])
        p_ref[0, 0:rows, COL_KK:COL_KK + RWKV_WIDTH] = kq * lax.rsqrt(jnp.maximum(ss, 1e-24))

    @pl.when(j < N_LAT_STEPS)
    def _():
        prev_ok = jnp.where(j >= 1, 1.0, 0.0)
        next_ok = jnp.where(j <= N_LAT_STEPS - 2, 1.0, 0.0)
        h_ext = jnp.concatenate(
            [_norm_mod(xp_ref[0], g1_ref, modb_ref) * prev_ok, _norm_mod(x_ref[0], g1_ref, modb_ref),
             _norm_mod(xn_ref[0], g1_ref, modb_ref) * next_ok], axis=0).astype(BF16)
        xf_ref[0] = _dot(h_ext[SUBLANES:SUBLANES + MM_TILE], w_ref[:, FOURIER_START:GATE_START]).astype(BF16)
        n_ext = MM_TILE + 2 * SUBLANES
        p_ext = _dot(h_ext, w_ref[:, 0:RWKV_COLS])
        pm = p_ext[SUBLANES:SUBLANES + MM_TILE]
        pu = pltpu.roll(p_ext, 1, 0)[SUBLANES:SUBLANES + MM_TILE]
        pd = pltpu.roll(p_ext, n_ext - 1, 0)[SUBLANES:SUBLANES + MM_TILE]
        p_ref[0, :, 0:RWKV_COLS] = pm + mup_ref[...] * (pu - pm) + mun_ref[...] * (pd - pm)
        normalised_keys(MM_TILE)

    @pl.when(j == N_LAT_STEPS)
    def _():
        hc = _norm_mod(c_ref[0], g1_ref, modc_ref).astype(BF16)
        pm = _dot(hc, w_ref[:, 0:RWKV_COLS])
        t = lax.broadcasted_iota(jnp.int32, (CTX_LEN, RWKV_COLS), 0)
        pu = jnp.where(t == 0, 0.0, pltpu.roll(pm, 1, 0))
        pd = jnp.where(t == CTX_LEN - 1, 0.0, pltpu.roll(pm, CTX_LEN - 1, 0))
        p_ref[0, 0:CTX_LEN, 0:RWKV_COLS] = pm + mup_ref[...] * (pu - pm) + mun_ref[...] * (pd - pm)
        normalised_keys(CTX_LEN)


def _inproj(x, ctx, mod3, g1, w_rf, mu_prev, mu_next, k_k, eh):
    halo_blocks = MM_TILE // SUBLANES
    last_halo = SEQ // SUBLANES - 1
    lat = lambda j: jnp.minimum(j, N_LAT_STEPS - 1)
    return pl.pallas_call(
        _inproj_kernel,
        out_shape=(jax.ShapeDtypeStruct((BATCH, SEQ_ALL, SLAB_COLS), F32),
                   jax.ShapeDtypeStruct((BATCH, SEQ, FOURIER_WIDTH), BF16)),
        grid=(BATCH, N_LAT_STEPS + 1),
        in_specs=[
            pl.BlockSpec((1, MM_TILE, D_MODEL), lambda b, j: (b, lat(j), 0)),
            pl.BlockSpec((1, SUBLANES, D_MODEL), lambda b, j: (b, jnp.maximum(lat(j) * halo_blocks - 1, 0), 0)),
            pl.BlockSpec((1, SUBLANES, D_MODEL),
                         lambda b, j: (b, jnp.minimum((lat(j) + 1) * halo_blocks, last_halo), 0)),
            pl.BlockSpec((1, CTX_LEN, D_MODEL), lambda b, j: (b, 0, 0)),
            pl.BlockSpec((1, 1, 2 * D_MODEL), lambda b, j: (b, 0, 0)),
            pl.BlockSpec((1, 1, 2 * D_MODEL), lambda b, j: (BATCH, 0, 0)),
            _const_spec((1, D_MODEL)),
            _const_spec((D_MODEL, GATE_START)),
            _const_spec((1, RWKV_COLS)),
            _const_spec((1, RWKV_COLS)),
            _const_spec((1, RWKV_WIDTH)),
            _const_spec((RWKV_WIDTH, RWKV_WIDTH)),
        ],
        out_specs=(
            pl.BlockSpec((1, MM_TILE, SLAB_COLS), lambda b, j: (b, j, 0)),
            pl.BlockSpec((1, MM_TILE, FOURIER_WIDTH), lambda b, j: (b, lat(j), 0)),
        ),
        compiler_params=pltpu.CompilerParams(
            dimension_semantics=("arbitrary", "arbitrary"), vmem_limit_bytes=VMEM_LIMIT),
        name="inproj_shift",
    )(x, x, x, ctx, mod3, mod3, g1, w_rf, mu_prev, mu_next, k_k, eh)


_S_KD, _S_BE, _S_LD, _S_LGI = range(4)
_N_PLANES = 4
_V_W0F, _V_W0B, _V_A0F, _V_A0B, _V_KA, _V_RK = range(6)


def _wkv_kernel(pf_ref, pb_ref, wlo_ref, alo_ref, vec_ref, esel_ref, tri_ref,
                yf_ref, yb_ref, bsf_ref, bsb_ref, zt_ref, s_ref):
    j = pl.program_id(1)

    @pl.when(j == 0)
    def _():
        zt_ref[...] = jnp.zeros_like(zt_ref)

    k_a = vec_ref[_V_KA:_V_KA + 1, :]
    r_k = vec_ref[_V_RK:_V_RK + 1, :]
    p_refs = (pf_ref, pb_ref)

    for d, bs_ref in enumerate((bsf_ref, bsb_ref)):
        p_ref = p_refs[d]
        w0 = vec_ref[_V_W0F + d:_V_W0F + d + 1, :]
        a0 = vec_ref[_V_A0F + d:_V_A0F + d + 1, :]
        k = p_ref[0, :, COL_K:COL_K + RWKV_WIDTH]
        wa = p_ref[0, :, COL_WD:COL_WD + LANES]
        aa = p_ref[0, :, COL_AD:COL_AD + LANES]
        wl = w0 + _dot(jnp.tanh(wa).astype(BF16), wlo_ref[d])
        ld = -DECAY_SCALE * _sigmoid(wl)
        a = _sigmoid(a0 + _dot(aa.astype(BF16), alo_ref[d]))
        kd = k * (1.0 + (a - 1.0) * k_a)
        base = d * _N_PLANES
        s_ref[base + _S_KD] = kd
        s_ref[base + _S_BE] = p_ref[0, :, COL_KK:COL_KK + RWKV_WIDTH] * a
        s_ref[base + _S_LD] = ld
        s_ref[base + _S_LGI] = _dot_split3(tri_ref[d], ld)
        bs_ref[0] = _dot((p_ref[0, :, COL_R:COL_R + RWKV_WIDTH] * kd * r_k).astype(BF16), esel_ref[...])

    ri = lax.broadcasted_iota(jnp.int32, (CHUNK, LANES), 0)
    li = lax.broadcasted_iota(jnp.int32, (CHUNK, LANES), 1)
    ci = jnp.bitwise_and(li, HEAD_DIM - 1)
    lane_lo = li < HEAD_DIM
    eye = ri == ci
    strict = (ci < ri, ci > ri)
    incl = (ci <= ri, ci >= ri)
    merge_masks = [jnp.right_shift(ri, 1) == jnp.right_shift(ci, 1)]
    for sh in range(1, CHUNK.bit_length() - 1):
        merge_masks.append(jnp.logical_and(jnp.right_shift(ri, sh + 1) == jnp.right_shift(ci, sh + 1),
                                           jnp.right_shift(ri, sh) != jnp.right_shift(ci, sh)))
    n2 = 2 * CHUNK
    same_head = (lax.broadcasted_iota(jnp.int32, (n2, LANES), 0) >= CHUNK) == \
        (lax.broadcasted_iota(jnp.int32, (n2, LANES), 1) >= HEAD_DIM)

    def stack(u):
        return jnp.concatenate([jnp.where(lane_lo, u, 0.0), jnp.where(lane_lo, 0.0, u)], axis=0)

    def load(d, p, r0):
        base = d * _N_PLANES
        rows = slice(r0, r0 + CHUNK)
        plane = lambda pln: s_ref[base + pln, rows, p * LANES:(p + 1) * LANES]
        col = lambda c0: p_refs[d][0, rows, c0 + p * LANES:c0 + (p + 1) * LANES]
        return (plane(_S_LD), plane(_S_LGI), col(COL_R), col(COL_V), plane(_S_KD), -col(COL_KK), plane(_S_BE))

    def phase_a(dirs, ins):
        us = range(len(dirs))
        ld, lgi, r, v, kd, al, be = ([ins[u][f] for u in us] for f in range(7))
        m = [lgi[u][CHUNK // 2:CHUNK // 2 + 1, :] for u in us]
        last = [CHUNK - 1 if dirs[u] == 0 else 0 for u in us]
        lgc = [lgi[u][last[u]:last[u] + 1, :] for u in us]
        lge = [lgi[u] - ld[u] for u in us]
        e_m = [jnp.exp(m[u] - lgi[u]) for u in us]
        lhs = [jnp.concatenate([al[u] * jnp.exp(lge[u] - m[u]), r[u] * jnp.exp(lgi[u] - m[u])], axis=0).astype(BF16)
               for u in us]
        rhs = [jnp.concatenate([stack(be[u] * e_m[u]), stack(kd[u] * e_m[u])], axis=0).astype(BF16) for u in us]
        sc = [_dot_nt(lhs[u], rhs[u]) for u in us]
        l_ab = [jnp.where(strict[dirs[u]], sc[u][0:CHUNK, 0:LANES], 0.0) for u in us]

        t_m = [jnp.where(eye, 1.0, jnp.where(merge_masks[0], l_ab[u], 0.0)) for u in us]
        for mk in merge_masks[1:]:
            t_bd = [stack(t_m[u]).astype(BF16) for u in us]
            e_l = [_dot(jnp.where(mk, l_ab[u], 0.0).astype(BF16), t_bd[u]) for u in us]
            t_m = [t_m[u] + _dot(t_m[u].astype(BF16), stack(e_l[u]).astype(BF16)) for u in us]
        t_b = [t_m[u].astype(BF16) for u in us]

        v2 = [stack(v[u]).astype(BF16) for u in us]
        nv = [_dot(jnp.concatenate([jnp.where(strict[dirs[u]], sc[u][0:CHUNK, LANES:2 * LANES], 0.0),
                                    jnp.where(incl[dirs[u]], sc[u][CHUNK:n2, LANES:2 * LANES], 0.0)],
                                   axis=0).astype(BF16), v2[u]) for u in us]
        wu = [_dot(t_b[u], jnp.concatenate([stack(al[u] * jnp.exp(lge[u])), stack(nv[u][0:CHUNK])],
                                           axis=1).astype(BF16)) for u in us]
        m_rb = [jnp.where(incl[dirs[u]], sc[u][CHUNK:n2, 0:LANES], 0.0).astype(BF16) for u in us]
        qy = [_dot(m_rb[u], jnp.concatenate([stack(wu[u][:, 0:LANES]), stack(wu[u][:, LANES:2 * LANES])],
                                            axis=1).astype(BF16)) for u in us]
        q = [(r[u] * jnp.exp(lgi[u]) + qy[u][:, 0:LANES]).astype(BF16) for u in us]
        y0 = [nv[u][CHUNK:n2] + qy[u][:, LANES:2 * LANES] for u in us]
        e_c = [jnp.exp(lgc[u] - lgi[u]) for u in us]
        b_e = [(be[u] * e_c[u]).astype(BF16) for u in us]
        g_l = [jnp.where(same_head, _dot_tn(wu[u][:, 0:LANES].astype(BF16), b_e[u]), 0.0).astype(BF16)
               for u in us]
        h_f = [_dot_tn(jnp.concatenate([wu[u][:, LANES:2 * LANES], v[u]], axis=0).astype(BF16),
                       jnp.concatenate([b_e[u], (kd[u] * e_c[u]).astype(BF16)], axis=0)) for u in us]
        h_t = [jnp.where(lane_lo, h_f[u][0:CHUNK], h_f[u][CHUNK:n2]) for u in us]
        g_c = [jnp.exp(lgc[u]) for u in us]
        return q, y0, g_l, h_t, g_c

    n_chunks = TOK_TILE // CHUNK
    pairs = HEADS // 2
    units = [(d, p, i if d == 0 else n_chunks - 1 - i)
             for i in range(n_chunks) for d in range(2) for p in range(pairs)]
    ins = [load(d, p, c * CHUNK) for d, p, c in units]
    q, y0, g_l, h_t, g_c = phase_a([d for d, _, _ in units], ins)
    y_refs = (yf_ref, yb_ref)
    z = {(d, p): zt_ref[d, p] for d in range(2) for p in range(pairs)}
    for u, (d, p, c) in enumerate(units):
        y_refs[d][0, c * CHUNK:(c + 1) * CHUNK, p * LANES:(p + 1) * LANES] = \
            _dot_nt(q[u], stack(z[d, p]).astype(BF16)) + y0[u]
        z[d, p] = z[d, p] * g_c[u] + _dot(z[d, p].astype(BF16), g_l[u]) + h_t[u]
    for (d, p), val in z.items():
        zt_ref[d, p] = val


def _wkv(p_all, wlo, alo, vecs, esel, tri):
    def fwd_tile(j):
        return jnp.where(j == 0, N_LAT_TILES, j - 1)

    def bwd_tile(j):
        return jnp.where(j == 0, N_LAT_TILES, N_LAT_TILES - j)

    def bwd_out(j):
        return jnp.where(j == 0, N_LAT_TILES - 1, N_LAT_TILES - j)

    y_shape = jax.ShapeDtypeStruct((BATCH, SEQ, RWKV_WIDTH), F32)
    bs_shape = jax.ShapeDtypeStruct((BATCH, SEQ, LANES), F32)
    return pl.pallas_call(
        _wkv_kernel,
        out_shape=(y_shape, y_shape, bs_shape, bs_shape),
        grid=(BATCH, N_TILES),
        in_specs=[
            pl.BlockSpec((1, TOK_TILE, SLAB_COLS), lambda b, j: (b, fwd_tile(j), 0)),
            pl.BlockSpec((1, TOK_TILE, SLAB_COLS), lambda b, j: (b, bwd_tile(j), 0)),
            _const_spec((2, LANES, RWKV_WIDTH)),
            _const_spec((2, LANES, RWKV_WIDTH)),
            _const_spec((SUBLANES, RWKV_WIDTH)),
            _const_spec((RWKV_WIDTH, LANES)),
            _const_spec((2, TOK_TILE, TOK_TILE)),
        ],
        out_specs=(
            pl.BlockSpec((1, TOK_TILE, RWKV_WIDTH), lambda b, j: (b, jnp.maximum(j - 1, 0), 0)),
            pl.BlockSpec((1, TOK_TILE, RWKV_WIDTH), lambda b, j: (b, bwd_out(j), 0)),
            pl.BlockSpec((1, TOK_TILE, LANES), lambda b, j: (b, jnp.maximum(j - 1, 0), 0)),
            pl.BlockSpec((1, TOK_TILE, LANES), lambda b, j: (b, bwd_out(j), 0)),
        ),
        scratch_shapes=[
            pltpu.VMEM((2, HEADS // 2, HEAD_DIM, LANES), F32),
            pltpu.VMEM((2 * _N_PLANES, TOK_TILE, RWKV_WIDTH), F32),
        ],
        compiler_params=pltpu.CompilerParams(
            dimension_semantics=("arbitrary", "arbitrary"), vmem_limit_bytes=VMEM_LIMIT),
        name="wkv7_chunked",
    )(p_all, p_all, wlo, alo, vecs, esel, tri)


GRID_H = SEQ // GRID_W
HALF_ROWS = GRID_H // 2 + 1
HALF_TOK = HALF_ROWS * GRID_W
MIRROR_TOK = SEQ - HALF_TOK


def _fourier_kernel(x_ref, cd_ref, ab_ref, jc_ref, jj_ref, o_ref, rhs_ref, m_ref):
    for g in range(FOURIER_GROUPS):
        gs = slice(g * FOURIER_GROUP_DIM, (g + 1) * FOURIER_GROUP_DIM)
        z = _dot(x_ref[0, :, gs], cd_ref[...])
        rhs_ref[0:SEQ, gs] = z[:, 0:FOURIER_GROUP_DIM].astype(BF16)
        rhs_ref[SEQ:2 * SEQ, gs] = z[:, FOURIER_GROUP_DIM:2 * FOURIER_GROUP_DIM].astype(BF16)
    yh = _dot(ab_ref[...], rhs_ref[...]).astype(BF16)
    o_ref[0, 0:HALF_TOK, :] = yh
    for k in range(GRID_H - HALF_ROWS):
        src = GRID_H - HALF_ROWS - k
        m_ref[k * GRID_W:(k + 1) * GRID_W, :] = _dot(
            jc_ref[...], yh[src * GRID_W:(src + 1) * GRID_W, :]).astype(BF16)
    o_ref[0, HALF_TOK:SEQ, :] = _dot(m_ref[...], jj_ref[...]).astype(BF16)


def _fourier(xf, cd, ab, jc, jj):
    return pl.pallas_call(
        _fourier_kernel,
        out_shape=jax.ShapeDtypeStruct((BATCH, SEQ, FOURIER_WIDTH), BF16),
        grid=(BATCH,),
        in_specs=[
            pl.BlockSpec((1, SEQ, FOURIER_WIDTH), lambda b: (b, 0, 0)),
            _const_spec((FOURIER_GROUP_DIM, 2 * FOURIER_GROUP_DIM)),
            _const_spec((HALF_TOK, 2 * SEQ)),
            _const_spec((GRID_W, GRID_W)),
            _const_spec((FOURIER_WIDTH, FOURIER_WIDTH)),
        ],
        out_specs=pl.BlockSpec((1, SEQ, FOURIER_WIDTH), lambda b: (b, 0, 0)),
        scratch_shapes=[pltpu.VMEM((2 * SEQ, FOURIER_WIDTH), BF16),
                        pltpu.VMEM((MIRROR_TOK, FOURIER_WIDTH), BF16)],
        compiler_params=pltpu.CompilerParams(vmem_limit_bytes=VMEM_LIMIT),
        name="fourier_dft",
    )(xf, cd, ab, jc, jj)


def _merge_kernel(x_ref, mod_ref, yf_ref, yb_ref, bsf_ref, bsb_ref, v_ref, gd_ref, fo_ref,
                  g1_ref, wg_ref, g2_ref, wur_ref, wuf_ref, wo_ref, ehm_ref, ex_ref, lng_ref, lnb_ref, o_ref):
    x = x_ref[0]
    shift = mod_ref[0, :, 0:D_MODEL]
    scale = mod_ref[0, :, D_MODEL:2 * D_MODEL]
    gate1 = mod_ref[0, :, 2 * D_MODEL:3 * D_MODEL]
    hx = ((_rms(x) * g1_ref[...]) * (1.0 + scale) + shift).astype(BF16)
    gates = _dot(hx, wg_ref[...])

    y = yf_ref[0] + yb_ref[0]
    mean = _dot(y.astype(BF16), ehm_ref[...])
    dy = y - mean
    var = _dot((dy * dy).astype(BF16), ehm_ref[...])
    o = dy * lax.rsqrt(var + GN_EPS) * lng_ref[...] + lnb_ref[...]
    bonus = _dot((bsf_ref[0] + bsb_ref[0]).astype(BF16), ex_ref[...]) * v_ref[0]
    g = _dot(_sigmoid(gd_ref[0]).astype(BF16), g2_ref[...])
    o = ((o + bonus) * g).astype(BF16)
    r_up = _dot(o, wur_ref[...])
    f_up = _dot(fo_ref[0], wuf_ref[...])
    mix = (_sigmoid(gates[:, 0:D_MODEL]) * f_up + _sigmoid(gates[:, D_MODEL:2 * D_MODEL]) * r_up).astype(BF16)
    o_ref[0] = x + gate1 * _dot(mix, wo_ref[...])


def _merge(x, mod3, yf, yb, bsf, bsb, p_all, fo, g1, wg, g2, wur, wuf, wo, ehm, ex, lng, lnb):
    tok = lambda w: pl.BlockSpec((1, MM_TILE, w), lambda b, t: (b, t, 0))
    return pl.pallas_call(
        _merge_kernel,
        out_shape=jax.ShapeDtypeStruct((BATCH, SEQ, D_MODEL), F32),
        grid=(BATCH, SEQ // MM_TILE),
        in_specs=[
            tok(D_MODEL),
            pl.BlockSpec((1, 1, 6 * D_MODEL), lambda b, t: (b, 0, 0)),
            tok(RWKV_WIDTH), tok(RWKV_WIDTH), tok(LANES), tok(LANES),
            pl.BlockSpec((1, MM_TILE, RWKV_WIDTH), lambda b, t: (b, t, COL_V // RWKV_WIDTH)),
            pl.BlockSpec((1, MM_TILE, GATE_LORA), lambda b, t: (b, t, COL_GD // GATE_LORA)),
            tok(FOURIER_WIDTH),
            _const_spec((1, D_MODEL)),
            _const_spec((D_MODEL, 2 * D_MODEL)),
            _const_spec((GATE_LORA, RWKV_WIDTH)),
            _const_spec((RWKV_WIDTH, D_MODEL)),
            _const_spec((FOURIER_WIDTH, D_MODEL)),
            _const_spec((D_MODEL, D_MODEL)),
            _const_spec((RWKV_WIDTH, RWKV_WIDTH)),
            _const_spec((LANES, RWKV_WIDTH)),
            _const_spec((1, RWKV_WIDTH)),
            _const_spec((1, RWKV_WIDTH)),
        ],
        out_specs=tok(D_MODEL),
        compiler_params=pltpu.CompilerParams(
            dimension_semantics=("arbitrary", "arbitrary"), vmem_limit_bytes=VMEM_LIMIT),
        name="branch_merge",
    )(x, mod3, yf, yb, bsf, bsb, p_all, p_all, fo, g1, wg, g2, wur, wuf, wo, ehm, ex, lng, lnb)


def _ffn_kernel(x_ref, mod_ref, g2_ref, wgu_ref, wd_ref, gf_ref, o_ref):
    x = x_ref[0]
    shift = mod_ref[0, :, 3 * D_MODEL:4 * D_MODEL]
    scale = mod_ref[0, :, 4 * D_MODEL:5 * D_MODEL]
    gate2 = mod_ref[0, :, 5 * D_MODEL:6 * D_MODEL]
    hx = ((_rms(x) * g2_ref[...]) * (1.0 + scale) + shift).astype(BF16)
    part = D_FF // FFN_SPLIT
    acc = jnp.zeros((FFN_TILE, D_MODEL), F32)
    for s in range(FFN_SPLIT):
        gt = _dot(hx, wgu_ref[:, s * part:(s + 1) * part])
        up = _dot(hx, wgu_ref[:, D_FF + s * part:D_FF + (s + 1) * part])
        h = (gt * _sigmoid(gt) * up).astype(BF16)
        acc = acc + _dot(h, wd_ref[s * part:(s + 1) * part, :])
    o_ref[0] = _rms(x + gate2 * acc) * gf_ref[...]


def _ffn(x1, mod3, g2, wgu, wd, gf):
    tiles = SEQ // FFN_TILE
    return pl.pallas_call(
        _ffn_kernel,
        out_shape=jax.ShapeDtypeStruct((BATCH, SEQ, D_MODEL), F32),
        grid=(BATCH, tiles),
        in_specs=[
            pl.BlockSpec((1, FFN_TILE, D_MODEL), lambda b, t: (b, t, 0)),
            pl.BlockSpec((1, 1, 6 * D_MODEL), lambda b, t: (b, 0, 0)),
            _const_spec((1, D_MODEL)),
            _const_spec((D_MODEL, 2 * D_FF)),
            _const_spec((D_FF, D_MODEL)),
            _const_spec((1, D_MODEL)),
        ],
        out_specs=pl.BlockSpec((1, FFN_TILE, D_MODEL), lambda b, t: (b, t, 0)),
        compiler_params=pltpu.CompilerParams(
            dimension_semantics=("arbitrary", "arbitrary"), vmem_limit_bytes=VMEM_LIMIT),
        name="swiglu_final",
    )(x1, mod3, g2, wgu, wd, gf)


@functools.lru_cache(maxsize=None)
def _constants():
    def dft(n):
        ang = 2.0 * np.pi * (np.outer(np.arange(n), np.arange(n)) % n) / n
        return np.cos(ang), -np.sin(ang)

    a_ch, b_ch = dft(FOURIER_GROUP_DIM)
    a_col, b_col = dft(GRID_W)
    a_row, b_row = dft(GRID_H)
    norm = 1.0 / np.sqrt(GRID_H * GRID_W * FOURIER_GROUP_DIM)
    cd = np.concatenate([a_ch, b_ch], axis=1) * norm
    a_tok = np.kron(a_row, a_col) - np.kron(b_row, b_col)
    b_tok = np.kron(a_row, b_col) + np.kron(b_row, a_col)
    ab = np.concatenate([a_tok, -b_tok], axis=1)[0:HALF_TOK]
    neg = lambda n: (np.arange(n)[:, None] == (-np.arange(n)[None, :]) % n).astype(np.float32)
    jc = neg(GRID_W)
    jj = np.kron(np.eye(FOURIER_GROUPS), neg(FOURIER_GROUP_DIM))
    head = np.arange(RWKV_WIDTH) // HEAD_DIM
    eh = (head[:, None] == head[None, :]).astype(np.float32)
    esel = (head[:, None] == np.arange(LANES)[None, :]).astype(np.float32)
    i = np.arange(TOK_TILE)
    same_chunk = (i[:, None] // CHUNK) == (i[None, :] // CHUNK)
    tri = np.stack([same_chunk & (i[None, :] <= i[:, None]), same_chunk & (i[None, :] >= i[:, None])])
    f32 = lambda u: np.asarray(u, np.float32)
    return dict(cd=f32(cd), ab=f32(ab), jc=f32(jc), jj=f32(jj), eh=f32(eh), ehm=f32(eh / HEAD_DIM), esel=f32(esel), ex=f32(esel.T),
                tri=f32(tri))


def kernel(x, c, ctx, c_ctx, norm1_g, norm2_g, w_ada, b_ada, w_in, mu_prev, mu_next, w0_f, w2_f, a0_f, a2_f, w0_b, w2_b, a0_b, a2_b, g2, k_k, k_a, r_k, lnx_g, lnx_b, w_up_r, w_up_f, w_out, w_gu, w_down, final_norm_g):
    cst = {name: jnp.asarray(val).astype(BF16) for name, val in _constants().items()}
    row = lambda u: u.reshape(1, -1)
    cc = jnp.concatenate(
        [c, c_ctx[None, :], jnp.zeros((MOD_ROWS - BATCH - 1, D_MODEL), F32)], axis=0)
    mod = _modulation(cc, w_ada[0].astype(BF16), row(b_ada[0]))
    mod3 = mod.reshape(MOD_ROWS, 1, 6 * D_MODEL)

    w_in0 = w_in[0].astype(BF16)
    p_all, xf = _inproj(x, ctx, mod3, row(norm1_g[0]), w_in0[:, 0:GATE_START], row(mu_prev[0]), row(mu_next[0]),
                        row(k_k[0]), cst["eh"])

    zeros_lora = jnp.zeros((DECAY_LORA, RWKV_WIDTH), F32)
    wlo = jnp.stack([jnp.concatenate([w2_f[0], zeros_lora], 0), jnp.concatenate([zeros_lora, w2_b[0]], 0)]).astype(BF16)
    alo = jnp.stack([jnp.concatenate([a2_f[0], zeros_lora], 0), jnp.concatenate([zeros_lora, a2_b[0]], 0)]).astype(BF16)
    zero_row = jnp.zeros((RWKV_WIDTH,), F32)
    vecs = jnp.stack([w0_f[0], w0_b[0], a0_f[0], a0_b[0], k_a[0], r_k[0].reshape(-1), zero_row, zero_row])
    yf, yb, bsf, bsb = _wkv(p_all, wlo, alo, vecs, cst["esel"], cst["tri"])

    fo = _fourier(xf, cst["cd"], cst["ab"], cst["jc"], cst["jj"])

    x1 = _merge(x, mod3, yf, yb, bsf, bsb, p_all, fo, row(norm1_g[0]), w_in0[:, GATE_START:],
                g2[0].astype(BF16), w_up_r[0].astype(BF16), w_up_f[0].astype(BF16), w_out[0].astype(BF16),
                cst["ehm"], cst["ex"], row(lnx_g[0]), row(lnx_b[0]))
    return _ffn(x1, mod3, row(norm2_g[0]), w_gu[0].astype(BF16), w_down[0].astype(BF16), row(final_norm_g))
```

```python
import functools

import jax
import jax.numpy as jnp
import numpy as np
from jax import lax
from jax.experimental import pallas as pl
from jax.experimental.pallas import tpu as pltpu

D_MODEL = 1024
BATCH = 16
SEQ = 2048
GRID_W = 64
CTX_LEN = 256
FOURIER_WIDTH = 512
FOURIER_GROUPS = 4
FOURIER_GROUP_DIM = FOURIER_WIDTH // FOURIER_GROUPS
RWKV_WIDTH = 512
HEAD_DIM = 64
HEADS = RWKV_WIDTH // HEAD_DIM
DECAY_LORA = 64
AAA_LORA = 64
GATE_LORA = 128
RWKV_COLS = 3 * RWKV_WIDTH + 2 * DECAY_LORA + 2 * AAA_LORA + GATE_LORA
FOURIER_START = RWKV_COLS
GATE_START = RWKV_COLS + FOURIER_WIDTH
D_FF = 2816
NORM_EPS = 1e-6
GN_EPS = 64e-5
DECAY_SCALE = float(np.exp(-0.5))

COL_R, COL_K, COL_V = 0, RWKV_WIDTH, 2 * RWKV_WIDTH
COL_WD = 3 * RWKV_WIDTH
COL_AD = COL_WD + 2 * DECAY_LORA
COL_GD = COL_AD + 2 * AAA_LORA
COL_KK = RWKV_COLS
SLAB_COLS = RWKV_COLS + RWKV_WIDTH

LANES = 128
SUBLANES = 8
TOK_TILE = 256
MM_TILE = 512
CHUNK = 64
SEQ_ALL = SEQ + CTX_LEN
N_TILES = SEQ_ALL // TOK_TILE
N_LAT_TILES = SEQ // TOK_TILE
FFN_TILE = 512
FFN_SPLIT = 11
MOD_ROWS = 24
VMEM_LIMIT = 56 * 1024 * 1024

F32 = jnp.float32
BF16 = jnp.bfloat16


def _dot(a, b):
    return jnp.dot(a, b, preferred_element_type=F32)


def _dot_nt(a, b):
    return lax.dot_general(a, b, (((1,), (1,)), ((), ())), preferred_element_type=F32)


def _dot_tn(a, b):
    return lax.dot_general(a, b, (((0,), (0,)), ((), ())), preferred_element_type=F32)


def _dot_split3(e, x):
    x1 = x.astype(BF16)
    rem = x - x1.astype(F32)
    x2 = rem.astype(BF16)
    x3 = (rem - x2.astype(F32)).astype(BF16)
    return _dot(e, x1) + _dot(e, x2) + _dot(e, x3)


def _rms(u):
    return u * lax.rsqrt(jnp.mean(u * u, axis=-1, keepdims=True) + NORM_EPS)


def _sigmoid(z):
    return 1.0 / (1.0 + jnp.exp(-z))


def _const_spec(shape):
    nd = len(shape)
    return pl.BlockSpec(shape, lambda *_: (0,) * nd, pipeline_mode=pl.Buffered(1))


def _mod_kernel(c_ref, w_ref, b_ref, o_ref):
    c = c_ref[...]
    s = (c * _sigmoid(c)).astype(BF16)
    o_ref[...] = _dot(s, w_ref[...].astype(BF16)) + b_ref[...]


def _modulation(cc, w_ada, b_ada):
    n_blk = 4
    bn = (6 * D_MODEL) // n_blk
    return pl.pallas_call(
        _mod_kernel,
        out_shape=jax.ShapeDtypeStruct((MOD_ROWS, 6 * D_MODEL), F32),
        grid=(n_blk,),
        in_specs=[
            pl.BlockSpec((MOD_ROWS, D_MODEL), lambda n: (0, 0)),
            pl.BlockSpec((D_MODEL, bn), lambda n: (0, n)),
            pl.BlockSpec((1, bn), lambda n: (0, n)),
        ],
        out_specs=pl.BlockSpec((MOD_ROWS, bn), lambda n: (0, n)),
        name="adaln_mod",
    )(cc, w_ada, b_ada)


def _norm_mod(u, g1_ref, mod_ref):
    shift = mod_ref[0, :, 0:D_MODEL]
    scale = mod_ref[0, :, D_MODEL:2 * D_MODEL]
    return (_rms(u) * g1_ref[...]) * (1.0 + scale) + shift


N_LAT_STEPS = SEQ // MM_TILE


def _inproj_kernel(x_ref, xp_ref, xn_ref, c_ref, modb_ref, modc_ref, g1_ref, w_ref, mup_ref, mun_ref, kk_ref, eh_ref,
                   p_ref, xf_ref, hx_ref):
    j = pl.program_id(1)

    def normalised_keys(rows):
        kq = p_ref[0, 0:rows, COL_K:COL_K + RWKV_WIDTH] * kk_ref[...]
        ss = _dot((kq * kq).astype(BF16), eh_ref[...])
        p_ref[0, 0:rows, COL_KK:COL_KK + RWKV_WIDTH] = kq * lax.rsqrt(jnp.maximum(ss, 1e-24))

    @pl.when(j < N_LAT_STEPS)
    def _():
        prev_ok = jnp.where(j >= 1, 1.0, 0.0)
        next_ok = jnp.where(j <= N_LAT_STEPS - 2, 1.0, 0.0)
        h_ext = jnp.concatenate(
            [_norm_mod(xp_ref[0], g1_ref, modb_ref) * prev_ok, _norm_mod(x_ref[0], g1_ref, modb_ref),
             _norm_mod(xn_ref[0], g1_ref, modb_ref) * next_ok], axis=0).astype(BF16)
        hx_ref[0] = h_ext[SUBLANES:SUBLANES + MM_TILE]
        xf_ref[0] = _dot(h_ext[SUBLANES:SUBLANES + MM_TILE], w_ref[:, FOURIER_START:GATE_START]).astype(BF16)
        n_ext = MM_TILE + 2 * SUBLANES
        p_ext = _dot(h_ext, w_ref[:, 0:RWKV_COLS])
        pm = p_ext[SUBLANES:SUBLANES + MM_TILE]
        pu = pltpu.roll(p_ext, 1, 0)[SUBLANES:SUBLANES + MM_TILE]
        pd = pltpu.roll(p_ext, n_ext - 1, 0)[SUBLANES:SUBLANES + MM_TILE]
        p_ref[0, :, 0:RWKV_COLS] = pm + mup_ref[...] * (pu - pm) + mun_ref[...] * (pd - pm)
        normalised_keys(MM_TILE)

    @pl.when(j == N_LAT_STEPS)
    def _():
        hc = _norm_mod(c_ref[0], g1_ref, modc_ref).astype(BF16)
        pm = _dot(hc, w_ref[:, 0:RWKV_COLS])
        t = lax.broadcasted_iota(jnp.int32, (CTX_LEN, RWKV_COLS), 0)
        pu = jnp.where(t == 0, 0.0, pltpu.roll(pm, 1, 0))
        pd = jnp.where(t == CTX_LEN - 1, 0.0, pltpu.roll(pm, CTX_LEN - 1, 0))
        p_ref[0, 0:CTX_LEN, 0:RWKV_COLS] = pm + mup_ref[...] * (pu - pm) + mun_ref[...] * (pd - pm)
        normalised_keys(CTX_LEN)


def _inproj(x, ctx, mod3, g1, w_rf, mu_prev, mu_next, k_k, eh):
    halo_blocks = MM_TILE // SUBLANES
    last_halo = SEQ // SUBLANES - 1
    lat = lambda j: jnp.minimum(j, N_LAT_STEPS - 1)
    return pl.pallas_call(
        _inproj_kernel,
        out_shape=(jax.ShapeDtypeStruct((BATCH, SEQ_ALL, SLAB_COLS), F32),
                   jax.ShapeDtypeStruct((BATCH, SEQ, FOURIER_WIDTH), BF16),
                   jax.ShapeDtypeStruct((BATCH, SEQ, D_MODEL), BF16)),
        grid=(BATCH, N_LAT_STEPS + 1),
        in_specs=[
            pl.BlockSpec((1, MM_TILE, D_MODEL), lambda b, j: (b, lat(j), 0)),
            pl.BlockSpec((1, SUBLANES, D_MODEL), lambda b, j: (b, jnp.maximum(lat(j) * halo_blocks - 1, 0), 0)),
            pl.BlockSpec((1, SUBLANES, D_MODEL),
                         lambda b, j: (b, jnp.minimum((lat(j) + 1) * halo_blocks, last_halo), 0)),
            pl.BlockSpec((1, CTX_LEN, D_MODEL), lambda b, j: (b, 0, 0)),
            pl.BlockSpec((1, 1, 2 * D_MODEL), lambda b, j: (b, 0, 0)),
            pl.BlockSpec((1, 1, 2 * D_MODEL), lambda b, j: (BATCH, 0, 0)),
            _const_spec((1, D_MODEL)),
            _const_spec((D_MODEL, GATE_START)),
            _const_spec((1, RWKV_COLS)),
            _const_spec((1, RWKV_COLS)),
            _const_spec((1, RWKV_WIDTH)),
            _const_spec((RWKV_WIDTH, RWKV_WIDTH)),
        ],
        out_specs=(
            pl.BlockSpec((1, MM_TILE, SLAB_COLS), lambda b, j: (b, j, 0)),
            pl.BlockSpec((1, MM_TILE, FOURIER_WIDTH), lambda b, j: (b, lat(j), 0)),
            pl.BlockSpec((1, MM_TILE, D_MODEL), lambda b, j: (b, lat(j), 0)),
        ),
        compiler_params=pltpu.CompilerParams(
            dimension_semantics=("arbitrary", "arbitrary"), vmem_limit_bytes=VMEM_LIMIT),
        name="inproj_shift",
    )(x, x, x, ctx, mod3, mod3, g1, w_rf, mu_prev, mu_next, k_k, eh)


_S_KD, _S_BE, _S_LD, _S_LGI = range(4)
_N_PLANES = 4
_V_W0F, _V_W0B, _V_A0F, _V_A0B, _V_KA, _V_RK = range(6)


def _wkv_kernel(pf_ref, pb_ref, wlo_ref, alo_ref, vec_ref, esel_ref, tri_ref,
                yf_ref, yb_ref, bsf_ref, bsb_ref, zt_ref, s_ref):
    j = pl.program_id(1)

    @pl.when(j == 0)
    def _():
        zt_ref[...] = jnp.zeros_like(zt_ref)

    k_a = vec_ref[_V_KA:_V_KA + 1, :]
    r_k = vec_ref[_V_RK:_V_RK + 1, :]
    p_refs = (pf_ref, pb_ref)

    for d, bs_ref in enumerate((bsf_ref, bsb_ref)):
        p_ref = p_refs[d]
        w0 = vec_ref[_V_W0F + d:_V_W0F + d + 1, :]
        a0 = vec_ref[_V_A0F + d:_V_A0F + d + 1, :]
        k = p_ref[0, :, COL_K:COL_K + RWKV_WIDTH]
        wa = p_ref[0, :, COL_WD:COL_WD + LANES]
        aa = p_ref[0, :, COL_AD:COL_AD + LANES]
        wl = w0 + _dot(jnp.tanh(wa).astype(BF16), wlo_ref[d])
        ld = -DECAY_SCALE * _sigmoid(wl)
        a = _sigmoid(a0 + _dot(aa.astype(BF16), alo_ref[d]))
        kd = k * (1.0 + (a - 1.0) * k_a)
        base = d * _N_PLANES
        s_ref[base + _S_KD] = kd
        s_ref[base + _S_BE] = p_ref[0, :, COL_KK:COL_KK + RWKV_WIDTH] * a
        s_ref[base + _S_LD] = ld
        s_ref[base + _S_LGI] = _dot_split3(tri_ref[d], ld)
        bs_ref[0] = _dot((p_ref[0, :, COL_R:COL_R + RWKV_WIDTH] * kd * r_k).astype(BF16), esel_ref[...])

    ri = lax.broadcasted_iota(jnp.int32, (CHUNK, LANES), 0)
    li = lax.broadcasted_iota(jnp.int32, (CHUNK, LANES), 1)
    ci = jnp.bitwise_and(li, HEAD_DIM - 1)
    lane_lo = li < HEAD_DIM
    eye = ri == ci
    strict = (ci < ri, ci > ri)
    incl = (ci <= ri, ci >= ri)
    merge_masks = [jnp.right_shift(ri, 1) == jnp.right_shift(ci, 1)]
    for sh in range(1, CHUNK.bit_length() - 1):
        merge_masks.append(jnp.logical_and(jnp.right_shift(ri, sh + 1) == jnp.right_shift(ci, sh + 1),
                                           jnp.right_shift(ri, sh) != jnp.right_shift(ci, sh)))
    n2 = 2 * CHUNK
    same_head = (lax.broadcasted_iota(jnp.int32, (n2, LANES), 0) >= CHUNK) == \
        (lax.broadcasted_iota(jnp.int32, (n2, LANES), 1) >= HEAD_DIM)

    def stack(u):
        return jnp.concatenate([jnp.where(lane_lo, u, 0.0), jnp.where(lane_lo, 0.0, u)], axis=0)

    def load(d, p, r0):
        base = d * _N_PLANES
        rows = slice(r0, r0 + CHUNK)
        plane = lambda pln: s_ref[base + pln, rows, p * LANES:(p + 1) * LANES]
        col = lambda c0: p_refs[d][0, rows, c0 + p * LANES:c0 + (p + 1) * LANES]
        return (plane(_S_LD), plane(_S_LGI), col(COL_R), col(COL_V), plane(_S_KD), -col(COL_KK), plane(_S_BE))

    def phase_a(dirs, ins):
        us = range(len(dirs))
        ld, lgi, r, v, kd, al, be = ([ins[u][f] for u in us] for f in range(7))
        m = [lgi[u][CHUNK // 2:CHUNK // 2 + 1, :] for u in us]
        last = [CHUNK - 1 if dirs[u] == 0 else 0 for u in us]
        lgc = [lgi[u][last[u]:last[u] + 1, :] for u in us]
        lge = [lgi[u] - ld[u] for u in us]
        e_m = [jnp.exp(m[u] - lgi[u]) for u in us]
        lhs = [jnp.concatenate([al[u] * jnp.exp(lge[u] - m[u]), r[u] * jnp.exp(lgi[u] - m[u])], axis=0).astype(BF16)
               for u in us]
        rhs = [jnp.concatenate([stack(be[u] * e_m[u]), stack(kd[u] * e_m[u])], axis=0).astype(BF16) for u in us]
        sc = [_dot_nt(lhs[u], rhs[u]) for u in us]
        l_ab = [jnp.where(strict[dirs[u]], sc[u][0:CHUNK, 0:LANES], 0.0) for u in us]

        t_m = [jnp.where(eye, 1.0, jnp.where(merge_masks[0], l_ab[u], 0.0)) for u in us]
        for mk in merge_masks[1:]:
            t_bd = [stack(t_m[u]).astype(BF16) for u in us]
            e_l = [_dot(jnp.where(mk, l_ab[u], 0.0).astype(BF16), t_bd[u]) for u in us]
            t_m = [t_m[u] + _dot(t_m[u].astype(BF16), stack(e_l[u]).astype(BF16)) for u in us]
        t_b = [t_m[u].astype(BF16) for u in us]

        v2 = [stack(v[u]).astype(BF16) for u in us]
        nv = [_dot(jnp.concatenate([jnp.where(strict[dirs[u]], sc[u][0:CHUNK, LANES:2 * LANES], 0.0),
                                    jnp.where(incl[dirs[u]], sc[u][CHUNK:n2, LANES:2 * LANES], 0.0)],
                                   axis=0).astype(BF16), v2[u]) for u in us]
        wu = [_dot(t_b[u], jnp.concatenate([stack(al[u] * jnp.exp(lge[u])), stack(nv[u][0:CHUNK])],
                                           axis=1).astype(BF16)) for u in us]
        m_rb = [jnp.where(incl[dirs[u]], sc[u][CHUNK:n2, 0:LANES], 0.0).astype(BF16) for u in us]
        qy = [_dot(m_rb[u], jnp.concatenate([stack(wu[u][:, 0:LANES]), stack(wu[u][:, LANES:2 * LANES])],
                                            axis=1).astype(BF16)) for u in us]
        q = [(r[u] * jnp.exp(lgi[u]) + qy[u][:, 0:LANES]).astype(BF16) for u in us]
        y0 = [nv[u][CHUNK:n2] + qy[u][:, LANES:2 * LANES] for u in us]
        e_c = [jnp.exp(lgc[u] - lgi[u]) for u in us]
        b_e = [(be[u] * e_c[u]).astype(BF16) for u in us]
        g_l = [jnp.where(same_head, _dot_tn(wu[u][:, 0:LANES].astype(BF16), b_e[u]), 0.0).astype(BF16)
               for u in us]
        h_f = [_dot_tn(jnp.concatenate([wu[u][:, LANES:2 * LANES], v[u]], axis=0).astype(BF16),
                       jnp.concatenate([b_e[u], (kd[u] * e_c[u]).astype(BF16)], axis=0)) for u in us]
        h_t = [jnp.where(lane_lo, h_f[u][0:CHUNK], h_f[u][CHUNK:n2]) for u in us]
        g_c = [jnp.exp(lgc[u]) for u in us]
        return q, y0, g_l, h_t, g_c

    n_chunks = TOK_TILE // CHUNK
    pairs = HEADS // 2
    units = [(d, p, i if d == 0 else n_chunks - 1 - i)
             for i in range(n_chunks) for d in range(2) for p in range(pairs)]
    ins = [load(d, p, c * CHUNK) for d, p, c in units]
    q, y0, g_l, h_t, g_c = phase_a([d for d, _, _ in units], ins)
    y_refs = (yf_ref, yb_ref)
    z = {(d, p): zt_ref[d, p] for d in range(2) for p in range(pairs)}
    for u, (d, p, c) in enumerate(units):
        y_refs[d][0, c * CHUNK:(c + 1) * CHUNK, p * LANES:(p + 1) * LANES] = \
            _dot_nt(q[u], stack(z[d, p]).astype(BF16)) + y0[u]
        z[d, p] = z[d, p] * g_c[u] + _dot(z[d, p].astype(BF16), g_l[u]) + h_t[u]
    for (d, p), val in z.items():
        zt_ref[d, p] = val


def _wkv(p_all, wlo, alo, vecs, esel, tri):
    def fwd_tile(j):
        return jnp.where(j == 0, N_LAT_TILES, j - 1)

    def bwd_tile(j):
        return jnp.where(j == 0, N_LAT_TILES, N_LAT_TILES - j)

    def bwd_out(j):
        return jnp.where(j == 0, N_LAT_TILES - 1, N_LAT_TILES - j)

    y_shape = jax.ShapeDtypeStruct((BATCH, SEQ, RWKV_WIDTH), F32)
    bs_shape = jax.ShapeDtypeStruct((BATCH, SEQ, LANES), F32)
    return pl.pallas_call(
        _wkv_kernel,
        out_shape=(y_shape, y_shape, bs_shape, bs_shape),
        grid=(BATCH, N_TILES),
        in_specs=[
            pl.BlockSpec((1, TOK_TILE, SLAB_COLS), lambda b, j: (b, fwd_tile(j), 0)),
            pl.BlockSpec((1, TOK_TILE, SLAB_COLS), lambda b, j: (b, bwd_tile(j), 0)),
            _const_spec((2, LANES, RWKV_WIDTH)),
            _const_spec((2, LANES, RWKV_WIDTH)),
            _const_spec((SUBLANES, RWKV_WIDTH)),
            _const_spec((RWKV_WIDTH, LANES)),
            _const_spec((2, TOK_TILE, TOK_TILE)),
        ],
        out_specs=(
            pl.BlockSpec((1, TOK_TILE, RWKV_WIDTH), lambda b, j: (b, jnp.maximum(j - 1, 0), 0)),
            pl.BlockSpec((1, TOK_TILE, RWKV_WIDTH), lambda b, j: (b, bwd_out(j), 0)),
            pl.BlockSpec((1, TOK_TILE, LANES), lambda b, j: (b, jnp.maximum(j - 1, 0), 0)),
            pl.BlockSpec((1, TOK_TILE, LANES), lambda b, j: (b, bwd_out(j), 0)),
        ),
        scratch_shapes=[
            pltpu.VMEM((2, HEADS // 2, HEAD_DIM, LANES), F32),
            pltpu.VMEM((2 * _N_PLANES, TOK_TILE, RWKV_WIDTH), F32),
        ],
        compiler_params=pltpu.CompilerParams(
            dimension_semantics=("arbitrary", "arbitrary"), vmem_limit_bytes=VMEM_LIMIT),
        name="wkv7_chunked",
    )(p_all, p_all, wlo, alo, vecs, esel, tri)


GRID_H = SEQ // GRID_W
HALF_ROWS = GRID_H // 2 + 1
HALF_TOK = HALF_ROWS * GRID_W
MIRROR_TOK = SEQ - HALF_TOK


def _fourier_kernel(x_ref, cd_ref, ab_ref, jc_ref, jj_ref, o_ref, rhs_ref, m_ref):
    for g in range(FOURIER_GROUPS):
        gs = slice(g * FOURIER_GROUP_DIM, (g + 1) * FOURIER_GROUP_DIM)
        z = _dot(x_ref[0, :, gs], cd_ref[...])
        rhs_ref[0:SEQ, gs] = z[:, 0:FOURIER_GROUP_DIM].astype(BF16)
        rhs_ref[SEQ:2 * SEQ, gs] = z[:, FOURIER_GROUP_DIM:2 * FOURIER_GROUP_DIM].astype(BF16)
    yh = _dot(ab_ref[...], rhs_ref[...]).astype(BF16)
    o_ref[0, 0:HALF_TOK, :] = yh
    for k in range(GRID_H - HALF_ROWS):
        src = GRID_H - HALF_ROWS - k
        m_ref[k * GRID_W:(k + 1) * GRID_W, :] = _dot(
            jc_ref[...], yh[src * GRID_W:(src + 1) * GRID_W, :]).astype(BF16)
    o_ref[0, HALF_TOK:SEQ, :] = _dot(m_ref[...], jj_ref[...]).astype(BF16)


def _fourier(xf, cd, ab, jc, jj):
    return pl.pallas_call(
        _fourier_kernel,
        out_shape=jax.ShapeDtypeStruct((BATCH, SEQ, FOURIER_WIDTH), BF16),
        grid=(BATCH,),
        in_specs=[
            pl.BlockSpec((1, SEQ, FOURIER_WIDTH), lambda b: (b, 0, 0)),
            _const_spec((FOURIER_GROUP_DIM, 2 * FOURIER_GROUP_DIM)),
            _const_spec((HALF_TOK, 2 * SEQ)),
            _const_spec((GRID_W, GRID_W)),
            _const_spec((FOURIER_WIDTH, FOURIER_WIDTH)),
        ],
        out_specs=pl.BlockSpec((1, SEQ, FOURIER_WIDTH), lambda b: (b, 0, 0)),
        scratch_shapes=[pltpu.VMEM((2 * SEQ, FOURIER_WIDTH), BF16),
                        pltpu.VMEM((MIRROR_TOK, FOURIER_WIDTH), BF16)],
        compiler_params=pltpu.CompilerParams(vmem_limit_bytes=VMEM_LIMIT),
        name="fourier_dft",
    )(xf, cd, ab, jc, jj)


def _merge_kernel(x_ref, hx_ref, mod_ref, yf_ref, yb_ref, bsf_ref, bsb_ref, v_ref, gd_ref, fo_ref,
                  wg_ref, g2_ref, wur_ref, wuf_ref, wo_ref, ehm_ref, ex_ref, lng_ref, lnb_ref, o_ref):
    x = x_ref[0]
    gate1 = mod_ref[0, :, 2 * D_MODEL:3 * D_MODEL]
    gates = _dot(hx_ref[0], wg_ref[...])

    y = yf_ref[0] + yb_ref[0]
    mean = _dot(y.astype(BF16), ehm_ref[...])
    dy = y - mean
    var = _dot((dy * dy).astype(BF16), ehm_ref[...])
    o = dy * lax.rsqrt(var + GN_EPS) * lng_ref[...] + lnb_ref[...]
    bonus = _dot((bsf_ref[0] + bsb_ref[0]).astype(BF16), ex_ref[...]) * v_ref[0]
    g = _dot(_sigmoid(gd_ref[0]).astype(BF16), g2_ref[...])
    o = ((o + bonus) * g).astype(BF16)
    r_up = _dot(o, wur_ref[...])
    f_up = _dot(fo_ref[0], wuf_ref[...])
    mix = (_sigmoid(gates[:, 0:D_MODEL]) * f_up + _sigmoid(gates[:, D_MODEL:2 * D_MODEL]) * r_up).astype(BF16)
    o_ref[0] = x + gate1 * _dot(mix, wo_ref[...])


def _merge(x, hx, mod3, yf, yb, bsf, bsb, p_all, fo, wg, g2, wur, wuf, wo, ehm, ex, lng, lnb):
    tok = lambda w: pl.BlockSpec((1, MM_TILE, w), lambda b, t: (b, t, 0))
    return pl.pallas_call(
        _merge_kernel,
        out_shape=jax.ShapeDtypeStruct((BATCH, SEQ, D_MODEL), F32),
        grid=(BATCH, SEQ // MM_TILE),
        in_specs=[
            tok(D_MODEL),
            tok(D_MODEL),
            pl.BlockSpec((1, 1, 6 * D_MODEL), lambda b, t: (b, 0, 0)),
            tok(RWKV_WIDTH), tok(RWKV_WIDTH), tok(LANES), tok(LANES),
            pl.BlockSpec((1, MM_TILE, RWKV_WIDTH), lambda b, t: (b, t, COL_V // RWKV_WIDTH)),
            pl.BlockSpec((1, MM_TILE, GATE_LORA), lambda b, t: (b, t, COL_GD // GATE_LORA)),
            tok(FOURIER_WIDTH),
            _const_spec((D_MODEL, 2 * D_MODEL)),
            _const_spec((GATE_LORA, RWKV_WIDTH)),
            _const_spec((RWKV_WIDTH, D_MODEL)),
            _const_spec((FOURIER_WIDTH, D_MODEL)),
            _const_spec((D_MODEL, D_MODEL)),
            _const_spec((RWKV_WIDTH, RWKV_WIDTH)),
            _const_spec((LANES, RWKV_WIDTH)),
            _const_spec((1, RWKV_WIDTH)),
            _const_spec((1, RWKV_WIDTH)),
        ],
        out_specs=tok(D_MODEL),
        compiler_params=pltpu.CompilerParams(
            dimension_semantics=("arbitrary", "arbitrary"), vmem_limit_bytes=VMEM_LIMIT),
        name="branch_merge",
    )(x, hx, mod3, yf, yb, bsf, bsb, p_all, p_all, fo, wg, g2, wur, wuf, wo, ehm, ex, lng, lnb)


def _ffn_kernel(x_ref, mod_ref, g2_ref, wgu_ref, wd_ref, gf_ref, o_ref):
    x = x_ref[0]
    shift = mod_ref[0, :, 3 * D_MODEL:4 * D_MODEL]
    scale = mod_ref[0, :, 4 * D_MODEL:5 * D_MODEL]
    gate2 = mod_ref[0, :, 5 * D_MODEL:6 * D_MODEL]
    hx = ((_rms(x) * g2_ref[...]) * (1.0 + scale) + shift).astype(BF16)
    part = D_FF // FFN_SPLIT
    acc = jnp.zeros((FFN_TILE, D_MODEL), F32)
    for s in range(FFN_SPLIT):
        gt = _dot(hx, wgu_ref[:, s * part:(s + 1) * part])
        up = _dot(hx, wgu_ref[:, D_FF + s * part:D_FF + (s + 1) * part])
        h = (gt * _sigmoid(gt) * up).astype(BF16)
        acc = acc + _dot(h, wd_ref[s * part:(s + 1) * part, :])
    o_ref[0] = _rms(x + gate2 * acc) * gf_ref[...]


def _ffn(x1, mod3, g2, wgu, wd, gf):
    tiles = SEQ // FFN_TILE
    return pl.pallas_call(
        _ffn_kernel,
        out_shape=jax.ShapeDtypeStruct((BATCH, SEQ, D_MODEL), F32),
        grid=(BATCH, tiles),
        in_specs=[
            pl.BlockSpec((1, FFN_TILE, D_MODEL), lambda b, t: (b, t, 0)),
            pl.BlockSpec((1, 1, 6 * D_MODEL), lambda b, t: (b, 0, 0)),
            _const_spec((1, D_MODEL)),
            _const_spec((D_MODEL, 2 * D_FF)),
            _const_spec((D_FF, D_MODEL)),
            _const_spec((1, D_MODEL)),
        ],
        out_specs=pl.BlockSpec((1, FFN_TILE, D_MODEL), lambda b, t: (b, t, 0)),
        compiler_params=pltpu.CompilerParams(
            dimension_semantics=("arbitrary", "arbitrary"), vmem_limit_bytes=VMEM_LIMIT),
        name="swiglu_final",
    )(x1, mod3, g2, wgu, wd, gf)


@functools.lru_cache(maxsize=None)
def _constants():
    def dft(n):
        ang = 2.0 * np.pi * (np.outer(np.arange(n), np.arange(n)) % n) / n
        return np.cos(ang), -np.sin(ang)

    a_ch, b_ch = dft(FOURIER_GROUP_DIM)
    a_col, b_col = dft(GRID_W)
    a_row, b_row = dft(GRID_H)
    norm = 1.0 / np.sqrt(GRID_H * GRID_W * FOURIER_GROUP_DIM)
    cd = np.concatenate([a_ch, b_ch], axis=1) * norm
    a_tok = np.kron(a_row, a_col) - np.kron(b_row, b_col)
    b_tok = np.kron(a_row, b_col) + np.kron(b_row, a_col)
    ab = np.concatenate([a_tok, -b_tok], axis=1)[0:HALF_TOK]
    neg = lambda n: (np.arange(n)[:, None] == (-np.arange(n)[None, :]) % n).astype(np.float32)
    jc = neg(GRID_W)
    jj = np.kron(np.eye(FOURIER_GROUPS), neg(FOURIER_GROUP_DIM))
    head = np.arange(RWKV_WIDTH) // HEAD_DIM
    eh = (head[:, None] == head[None, :]).astype(np.float32)
    esel = (head[:, None] == np.arange(LANES)[None, :]).astype(np.float32)
    i = np.arange(TOK_TILE)
    same_chunk = (i[:, None] // CHUNK) == (i[None, :] // CHUNK)
    tri = np.stack([same_chunk & (i[None, :] <= i[:, None]), same_chunk & (i[None, :] >= i[:, None])])
    f32 = lambda u: np.asarray(u, np.float32)
    return dict(cd=f32(cd), ab=f32(ab), jc=f32(jc), jj=f32(jj), eh=f32(eh), ehm=f32(eh / HEAD_DIM), esel=f32(esel), ex=f32(esel.T),
                tri=f32(tri))


def kernel(x, c, ctx, c_ctx, norm1_g, norm2_g, w_ada, b_ada, w_in, mu_prev, mu_next, w0_f, w2_f, a0_f, a2_f, w0_b, w2_b, a0_b, a2_b, g2, k_k, k_a, r_k, lnx_g, lnx_b, w_up_r, w_up_f, w_out, w_gu, w_down, final_norm_g):
    cst = {name: jnp.asarray(val).astype(BF16) for name, val in _constants().items()}
    row = lambda u: u.reshape(1, -1)
    cc = jnp.concatenate(
        [c, c_ctx[None, :], jnp.zeros((MOD_ROWS - BATCH - 1, D_MODEL), F32)], axis=0)
    mod = _modulation(cc, w_ada[0], row(b_ada[0]))
    mod3 = mod.reshape(MOD_ROWS, 1, 6 * D_MODEL)

    p_all, xf, hx = _inproj(x, ctx, mod3, row(norm1_g[0]), w_in[0, :, 0:GATE_START].astype(BF16),
                            row(mu_prev[0]), row(mu_next[0]), row(k_k[0]), cst["eh"])

    zeros_lora = jnp.zeros((DECAY_LORA, RWKV_WIDTH), F32)
    wlo = jnp.stack([jnp.concatenate([w2_f[0], zeros_lora], 0), jnp.concatenate([zeros_lora, w2_b[0]], 0)]).astype(BF16)
    alo = jnp.stack([jnp.concatenate([a2_f[0], zeros_lora], 0), jnp.concatenate([zeros_lora, a2_b[0]], 0)]).astype(BF16)
    zero_row = jnp.zeros((RWKV_WIDTH,), F32)
    vecs = jnp.stack([w0_f[0], w0_b[0], a0_f[0], a0_b[0], k_a[0], r_k[0].reshape(-1), zero_row, zero_row])
    yf, yb, bsf, bsb = _wkv(p_all, wlo, alo, vecs, cst["esel"], cst["tri"])

    fo = _fourier(xf, cst["cd"], cst["ab"], cst["jc"], cst["jj"])

    x1 = _merge(x, hx, mod3, yf, yb, bsf, bsb, p_all, fo, w_in[0, :, GATE_START:].astype(BF16),
                g2[0].astype(BF16), w_up_r[0].astype(BF16), w_up_f[0].astype(BF16), w_out[0].astype(BF16),
                cst["ehm"], cst["ex"], row(lnx_g[0]), row(lnx_b[0]))
    return _ffn(x1, mod3, row(norm2_g[0]), w_gu[0].astype(BF16), w_down[0].astype(BF16), row(final_norm_g))
```
